```python
import math, functools
import jax, jax.numpy as jnp
from jax import lax
import numpy as np

D_MODEL = 1024
BATCH = 2
SEQ = 8192
DEPTH = 2
DEC_BATCH = 128
DEC_SEQ = 8
PAST_LEN = 16384
PAGE_SIZE = 128

EPS = 1e-6
NEG_INF = -1e30
MLA_HEADS = 8
Q_LORA = 384
KV_LORA = 256
QK_NOPE = 64
QK_ROPE = 32
V_HEAD = 64
ROPE_THETA = 10000.0
Q_BLOCK = 128
MLA_SCALE = 1.0 / math.sqrt(QK_NOPE + QK_ROPE)
SSM_HEADS = 8
SSM_HEAD_DIM = 64
D_INNER = SSM_HEADS * SSM_HEAD_DIM
SSM_GROUPS = 2
D_STATE = 64
CONV_K = 4
CONV_DIM = D_INNER + 2 * SSM_GROUPS * D_STATE
SSM_CHUNK = 128
GM_GROUPS = 8
GM_GROUP_DIM = 64
GM_WIDTH = GM_GROUPS * GM_GROUP_DIM
GM_CHUNK = 128
MEM_TOKENS = 256
MEM_HEADS = 4
MEM_HEAD_DIM = 64
MEM_WIDTH = MEM_HEADS * MEM_HEAD_DIM
MEM_SCALE = 1.0 / math.sqrt(MEM_HEAD_DIM)
D_FF = 2816
N_BRANCH = 3
IN_SIZES = (Q_LORA, KV_LORA, QK_ROPE, D_INNER, CONV_DIM, SSM_HEADS, GM_WIDTH, GM_WIDTH)
IN_DIM = sum(IN_SIZES)
IN_OFFSETS = tuple(sum(IN_SIZES[:i + 1]) for i in range(len(IN_SIZES) - 1))

kernel_name = 'hybrid_mla_ssd_chunkmlp_macaron_decode_step'


def rmsnorm(x, g):
    xf = x.astype(jnp.float32)
    xf = xf * lax.rsqrt(jnp.mean(xf * xf, axis=-1, keepdims=True) + EPS)
    return xf.astype(x.dtype) * g


def swiglu_ffn(h, w_in, w_out):
    gate, up = jnp.split(h @ w_in, 2, axis=-1)
    return (jax.nn.silu(gate) * up) @ w_out


def apply_rope(x, pos):
    half = QK_ROPE // 2
    inv = ROPE_THETA ** (-jnp.arange(half, dtype=jnp.float32) / half)
    ang = pos.astype(jnp.float32)[:, None] * inv[None, :]
    cos = jnp.cos(ang)[:, None, :]
    sin = jnp.sin(ang)[:, None, :]
    xf = x.astype(jnp.float32)
    x1, x2 = xf[..., :half], xf[..., half:]
    return jnp.concatenate([x1 * cos - x2 * sin, x2 * cos + x1 * sin], axis=-1).astype(x.dtype)


def latent_attention(q_lat, q_rope, c_kv, k_rope, q_pos, k_pos):
    s = (jnp.einsum('...qhr,...kr->...hqk', q_lat, c_kv)
         + jnp.einsum('...qhd,...kd->...hqk', q_rope, k_rope)).astype(jnp.float32) * MLA_SCALE
    s = jnp.where(k_pos[None, :] <= q_pos[:, None], s, NEG_INF)
    p = jax.nn.softmax(s, axis=-1).astype(c_kv.dtype)
    return jnp.einsum('...hqk,...kr->...qhr', p, c_kv)


def attend_prompt(q_lat, q_rope, c_kv, k_rope, pos):
    bsz, length, heads, rank = q_lat.shape
    blk = min(Q_BLOCK, length)
    nb = length // blk

    def to_blocks(t):
        return jnp.moveaxis(t.reshape(bsz, nb, blk, *t.shape[2:]), 1, 0)

    def one_block(args):
        ql, qr, qp = args
        return latent_attention(ql, qr, c_kv, k_rope, qp, pos)

    o = lax.map(one_block, (to_blocks(q_lat), to_blocks(q_rope), pos.reshape(nb, blk)))
    return jnp.moveaxis(o, 0, 1).reshape(bsz, length, heads, rank)


def attend_sample(q_lat, q_rope, c_kv, k_rope, cache_kv, cache_kr, layer, page_table, past):
    n_new = q_lat.shape[1]
    q_pos = past + jnp.arange(n_new, dtype=jnp.int32)
    k_pos = jnp.arange(past + n_new, dtype=jnp.int32)

    def one_seq(args):
        pt, ql, qr, cn, kn = args
        ckv = jnp.concatenate([cache_kv[layer, pt].reshape(-1, KV_LORA).astype(cn.dtype), cn], axis=0)
        kr = jnp.concatenate([cache_kr[layer, pt].reshape(-1, QK_ROPE).astype(kn.dtype), kn], axis=0)
        return latent_attention(ql, qr, ckv, kr, q_pos, k_pos)

    return lax.map(one_seq, (page_table, q_lat, q_rope, c_kv, k_rope))


def ssd_scan(x, dt, a, bm, cm, h0):
    bsz, length, heads, hdim = x.shape
    c = min(SSM_CHUNK, length)
    nc = length // c
    rep = heads // SSM_GROUPS
    f32 = jnp.float32
    bh = jnp.repeat(bm.astype(f32), rep, axis=2).reshape(bsz, nc, c, heads, D_STATE)
    ch = jnp.repeat(cm.astype(f32), rep, axis=2).reshape(bsz, nc, c, heads, D_STATE)
    xdt = (x.astype(f32) * dt[..., None]).reshape(bsz, nc, c, heads, hdim)
    acs = jnp.cumsum((dt * a).reshape(bsz, nc, c, heads), axis=2)
    causal = jnp.tril(jnp.ones((c, c), dtype=bool))
    seg = acs[:, :, :, None, :] - acs[:, :, None, :, :]
    decay = jnp.exp(jnp.where(causal[None, None, :, :, None], seg, NEG_INF))
    y_diag = jnp.einsum('bntsh,bnshp->bnthp', jnp.einsum('bnthk,bnshk->bntsh', ch, bh) * decay, xdt)
    to_end = jnp.exp(acs[:, :, -1:, :] - acs)
    chunk_states = jnp.einsum('bnshk,bnsh,bnshp->bnhpk', bh, to_end, xdt)
    chunk_decay = jnp.exp(acs[:, :, -1, :])

    def step(h, inp):
        s_c, d_c = inp
        return h * d_c[:, :, None, None] + s_c, h

    h_last, h_in = lax.scan(step, h0.astype(f32),
                            (jnp.moveaxis(chunk_states, 1, 0), jnp.moveaxis(chunk_decay, 1, 0)))
    h_in = jnp.moveaxis(h_in, 0, 1)
    y_off = jnp.einsum('bnthk,bnhpk,bnth->bnthp', ch, h_in, jnp.exp(acs))
    y = (y_diag + y_off).reshape(bsz, length, heads, hdim).astype(x.dtype)
    return y, h_last.astype(x.dtype)


def mamba_branch(z, xbc, dt_raw, conv_init, ssm_init, lp):
    bsz, length, _ = xbc.shape
    xpad = jnp.concatenate([conv_init.astype(xbc.dtype), xbc], axis=1)
    new_conv = xpad[:, length:]
    conv = sum(xpad[:, k:k + length] * lp['conv_w'][k] for k in range(CONV_K)) + lp['conv_b']
    xbc = jax.nn.silu(conv)
    gn = SSM_GROUPS * D_STATE
    xs = xbc[..., :D_INNER].reshape(bsz, length, SSM_HEADS, SSM_HEAD_DIM)
    bm = xbc[..., D_INNER:D_INNER + gn].reshape(bsz, length, SSM_GROUPS, D_STATE)
    cm = xbc[..., D_INNER + gn:].reshape(bsz, length, SSM_GROUPS, D_STATE)
    dt = jax.nn.softplus(dt_raw.astype(jnp.float32) + lp['dt_bias'].astype(jnp.float32))
    a = -jnp.exp(lp['a_log'].astype(jnp.float32))
    y, h_last = ssd_scan(xs, dt, a, bm, cm, ssm_init)
    y = (y + lp['d_skip'][:, None] * xs).reshape(bsz, length, D_INNER)
    y = rmsnorm(y * jax.nn.silu(z), lp['ssm_norm'])
    return y @ lp['w_o_ssm'], new_conv, h_last


def chunk_mlp_branch(u, v, lp):
    bsz, length, _ = u.shape
    c = min(GM_CHUNK, length)
    nc = length // c
    u = jax.nn.gelu(u)
    v = rmsnorm(jax.nn.gelu(v), lp['gm_norm'])
    w = jnp.tril(lp['w_spatial'][:, :c, :c])
    vb = v.reshape(bsz, nc, c, GM_GROUPS, GM_GROUP_DIM)
    s = jnp.einsum('gts,bnsgd->bntgd', w, vb) + lp['b_spatial'][:, :c].T[None, None, :, :, None]
    out = (u.reshape(bsz, nc, c, GM_GROUPS, GM_GROUP_DIM) * s).reshape(bsz, length, GM_WIDTH)
    return out @ lp['w_o_gm'], v


def memory_kv(mem, g, w_kv):
    bsz, m, _ = mem.shape
    k, v = jnp.split(rmsnorm(mem, g) @ w_kv, 2, axis=-1)
    return (k.reshape(bsz, m, MEM_HEADS, MEM_HEAD_DIM), v.reshape(bsz, m, MEM_HEADS, MEM_HEAD_DIM))


def memory_attention(hq, mem_k, mem_v, w_q, w_o):
    bsz, length, _ = hq.shape
    q = (hq @ w_q).reshape(bsz, length, MEM_HEADS, MEM_HEAD_DIM)
    s = jnp.einsum('blhd,bmhd->bhlm', q, mem_k.astype(q.dtype)).astype(jnp.float32) * MEM_SCALE
    p = jax.nn.softmax(s, axis=-1).astype(q.dtype)
    o = jnp.einsum('bhlm,bmhd->blhd', p, mem_v.astype(q.dtype)).reshape(bsz, length, MEM_WIDTH)
    return o @ w_o


def decoder_layer(x, pos, mem_k, mem_v, conv_init, ssm_init, attend_fn, lp):
    bsz, length, _ = x.shape
    x = x + 0.5 * swiglu_ffn(rmsnorm(x, lp['ffn_norm'][0]), lp['w_ffn_in'][0], lp['w_ffn_out'][0])
    h = rmsnorm(x, lp['mix_norm'])
    c_q, c_kv, k_r, z, xbc, dt_raw, u, v = jnp.split(h @ lp['w_in'], IN_OFFSETS, axis=-1)
    q = (rmsnorm(c_q, lp['q_norm']) @ lp['w_uq']).reshape(bsz, length, MLA_HEADS, QK_NOPE + QK_ROPE)
    q_rope = apply_rope(q[..., QK_NOPE:], pos)
    q_lat = jnp.einsum('blhd,hrd->blhr', q[..., :QK_NOPE], lp['w_uk'])
    c_kv = rmsnorm(c_kv, lp['kv_norm'])
    k_r = apply_rope(k_r[:, :, None, :], pos)[:, :, 0, :]
    o_lat = attend_fn(q_lat, q_rope, c_kv, k_r)
    o_mla = jnp.einsum('blhr,hrd->blhd', o_lat, lp['w_uv']).reshape(bsz, length, MLA_HEADS * V_HEAD) @ lp['w_o_mla']
    o_ssm, new_conv, new_ssm = mamba_branch(z, xbc, dt_raw, conv_init, ssm_init, lp)
    o_gm, v_rows = chunk_mlp_branch(u, v, lp)
    g_mla, g_ssm, g_gm = jnp.split(jax.nn.sigmoid(h @ lp['w_gate'] + lp['b_gate']), N_BRANCH, axis=-1)
    x = x + (g_mla * o_mla + g_ssm * o_ssm + g_gm * o_gm) @ lp['w_out']
    x = x + memory_attention(rmsnorm(x, lp['mem_q_norm']), mem_k, mem_v, lp['w_mem_q'], lp['w_mem_o'])
    x = x + 0.5 * swiglu_ffn(rmsnorm(x, lp['ffn_norm'][1]), lp['w_ffn_in'][1], lp['w_ffn_out'][1])
    return x, c_kv, k_r, new_conv, new_ssm, v_rows


def setup_inputs(seed: int = 0) -> dict:
    key = jax.random.key(seed)
    ks = iter(jax.random.split(key, 48))
    f32 = jnp.float32

    def nrm(shape, scale=1.0):
        return jax.random.normal(next(ks), shape, f32) * scale

    def gain(shape):
        return 1.0 + 0.02 * jax.random.normal(next(ks), shape, f32)

    n_pages = PAST_LEN // PAGE_SIZE
    n_used = DEC_BATCH * n_pages
    n_phys = n_used + max(1, n_used // 4)
    x_prompt = nrm((BATCH, SEQ, D_MODEL))
    x_sample = nrm((DEC_BATCH, DEC_SEQ, D_MODEL))
    mem_prompt = nrm((BATCH, MEM_TOKENS, D_MODEL))
    cache_kv_latent = nrm((DEPTH, n_phys, PAGE_SIZE, KV_LORA))
    cache_k_rope = nrm((DEPTH, n_phys, PAGE_SIZE, QK_ROPE))
    state_ssm = nrm((DEPTH, DEC_BATCH, SSM_HEADS, SSM_HEAD_DIM, D_STATE), 0.5)
    state_conv = nrm((DEPTH, DEC_BATCH, CONV_K - 1, CONV_DIM))
    cache_mem_k = nrm((DEPTH, DEC_BATCH, MEM_TOKENS, MEM_HEADS, MEM_HEAD_DIM))
    cache_mem_v = nrm((DEPTH, DEC_BATCH, MEM_TOKENS, MEM_HEADS, MEM_HEAD_DIM))
    page_table = jax.random.permutation(next(ks), n_phys)[:n_used].reshape(DEC_BATCH, n_pages).astype(jnp.int32)
    dt0 = jnp.exp(jax.random.uniform(next(ks), (DEPTH, SSM_HEADS), f32) * (math.log(0.1) - math.log(0.001)) + math.log(0.001))
    dt_bias = dt0 + jnp.log(-jnp.expm1(-dt0))
    a_log = jnp.log(jax.random.uniform(next(ks), (DEPTH, SSM_HEADS), f32, minval=1.0, maxval=16.0))
    return {
        'x_prompt': x_prompt,
        'x_sample': x_sample,
        'mem_prompt': mem_prompt,
        'cache_kv_latent': cache_kv_latent,
        'cache_k_rope': cache_k_rope,
        'state_ssm': state_ssm,
        'state_conv': state_conv,
        'cache_mem_k': cache_mem_k,
        'cache_mem_v': cache_mem_v,
        'page_table': page_table,
        'ffn_norm': gain((DEPTH, 2, D_MODEL)),
        'w_ffn_in': nrm((DEPTH, 2, D_MODEL, 2 * D_FF), D_MODEL ** -0.5),
        'w_ffn_out': nrm((DEPTH, 2, D_FF, D_MODEL), D_FF ** -0.5),
        'mix_norm': gain((DEPTH, D_MODEL)),
        'w_in': nrm((DEPTH, D_MODEL, IN_DIM), D_MODEL ** -0.5),
        'q_norm': gain((DEPTH, Q_LORA)),
        'w_uq': nrm((DEPTH, Q_LORA, MLA_HEADS * (QK_NOPE + QK_ROPE)), Q_LORA ** -0.5),
        'kv_norm': gain((DEPTH, KV_LORA)),
        'w_uk': nrm((DEPTH, MLA_HEADS, KV_LORA, QK_NOPE), KV_LORA ** -0.5),
        'w_uv': nrm((DEPTH, MLA_HEADS, KV_LORA, V_HEAD), KV_LORA ** -0.5),
        'w_o_mla': nrm((DEPTH, MLA_HEADS * V_HEAD, D_MODEL), (MLA_HEADS * V_HEAD) ** -0.5),
        'conv_w': nrm((DEPTH, CONV_K, CONV_DIM), CONV_K ** -0.5),
        'conv_b': nrm((DEPTH, CONV_DIM), 0.02),
        'dt_bias': dt_bias,
        'a_log': a_log,
        'd_skip': gain((DEPTH, SSM_HEADS)),
        'ssm_norm': gain((DEPTH, D_INNER)),
        'w_o_ssm': nrm((DEPTH, D_INNER, D_MODEL), D_INNER ** -0.5),
        'gm_norm': gain((DEPTH, GM_WIDTH)),
        'w_spatial': nrm((DEPTH, GM_GROUPS, GM_CHUNK, GM_CHUNK), GM_CHUNK ** -0.5),
        'b_spatial': gain((DEPTH, GM_GROUPS, GM_CHUNK)),
        'w_o_gm': nrm((DEPTH, GM_WIDTH, D_MODEL), GM_WIDTH ** -0.5),
        'w_gate': nrm((DEPTH, D_MODEL, N_BRANCH * D_MODEL), D_MODEL ** -0.5),
        'b_gate': nrm((DEPTH, N_BRANCH * D_MODEL), 0.02),
        'w_out': nrm((DEPTH, D_MODEL, D_MODEL), D_MODEL ** -0.5),
        'mem_q_norm': gain((DEPTH, D_MODEL)),
        'mem_kv_norm': gain((DEPTH, D_MODEL)),
        'w_mem_q': nrm((DEPTH, D_MODEL, MEM_WIDTH), D_MODEL ** -0.5),
        'w_mem_kv': nrm((DEPTH, D_MODEL, 2 * MEM_WIDTH), D_MODEL ** -0.5),
        'w_mem_o': nrm((DEPTH, MEM_WIDTH, D_MODEL), MEM_WIDTH ** -0.5),
        'final_norm': gain((D_MODEL,)),
    }


def reference(x_prompt, x_sample, mem_prompt, cache_kv_latent, cache_k_rope, state_ssm, state_conv,
              cache_mem_k, cache_mem_v, page_table, ffn_norm, w_ffn_in, w_ffn_out, mix_norm, w_in,
              q_norm, w_uq, kv_norm, w_uk, w_uv, w_o_mla, conv_w, conv_b, dt_bias, a_log, d_skip,
              ssm_norm, w_o_ssm, gm_norm, w_spatial, b_spatial, w_o_gm, w_gate, b_gate, w_out,
              mem_q_norm, mem_kv_norm, w_mem_q, w_mem_kv, w_mem_o, final_norm):
    bsz, seq, _ = x_prompt.shape
    dec_seq = x_sample.shape[1]
    past = page_table.shape[1] * cache_kv_latent.shape[2]
    pos_p = jnp.arange(seq, dtype=jnp.int32)
    pos_s = past + jnp.arange(dec_seq, dtype=jnp.int32)
    xp, xs = x_prompt, x_sample
    kvl_p, kr_p, ssm_p, conv_p, mk_p, mv_p = [], [], [], [], [], []
    kvl_s, kr_s, ssm_s, conv_s, gv_s = [], [], [], [], []
    for l in range(DEPTH):
        lp = {
            'ffn_norm': ffn_norm[l], 'w_ffn_in': w_ffn_in[l], 'w_ffn_out': w_ffn_out[l],
            'mix_norm': mix_norm[l], 'w_in': w_in[l],
            'q_norm': q_norm[l], 'w_uq': w_uq[l], 'kv_norm': kv_norm[l], 'w_uk': w_uk[l],
            'w_uv': w_uv[l], 'w_o_mla': w_o_mla[l],
            'conv_w': conv_w[l], 'conv_b': conv_b[l], 'dt_bias': dt_bias[l], 'a_log': a_log[l],
            'd_skip': d_skip[l], 'ssm_norm': ssm_norm[l], 'w_o_ssm': w_o_ssm[l],
            'gm_norm': gm_norm[l], 'w_spatial': w_spatial[l], 'b_spatial': b_spatial[l], 'w_o_gm': w_o_gm[l],
            'w_gate': w_gate[l], 'b_gate': b_gate[l], 'w_out': w_out[l],
            'mem_q_norm': mem_q_norm[l], 'w_mem_q': w_mem_q[l], 'w_mem_o': w_mem_o[l],
        }
        mem_k, mem_v = memory_kv(mem_prompt, mem_kv_norm[l], w_mem_kv[l])
        xp, ckv, kr, conv_new, ssm_new, _ = decoder_layer(
            xp, pos_p, mem_k, mem_v,
            jnp.zeros((bsz, CONV_K - 1, CONV_DIM), xp.dtype),
            jnp.zeros((bsz, SSM_HEADS, SSM_HEAD_DIM, D_STATE), jnp.float32),
            functools.partial(attend_prompt, pos=pos_p), lp)
        kvl_p.append(ckv)
        kr_p.append(kr)
        conv_p.append(conv_new)
        ssm_p.append(ssm_new)
        mk_p.append(mem_k)
        mv_p.append(mem_v)
        xs, ckv, kr, conv_new, ssm_new, v_rows = decoder_layer(
            xs, pos_s, cache_mem_k[l], cache_mem_v[l], state_conv[l], state_ssm[l],
            functools.partial(attend_sample, cache_kv=cache_kv_latent, cache_kr=cache_k_rope,
                              layer=l, page_table=page_table, past=past), lp)
        kvl_s.append(ckv)
        kr_s.append(kr)
        conv_s.append(conv_new)
        ssm_s.append(ssm_new)
        gv_s.append(v_rows)
    y_prompt = rmsnorm(xp, final_norm)
    y_sample = rmsnorm(xs, final_norm)
    return (y_prompt, y_sample,
            jnp.stack(kvl_p), jnp.stack(kr_p), jnp.stack(ssm_p), jnp.stack(conv_p),
            jnp.stack(mk_p), jnp.stack(mv_p),
            jnp.stack(kvl_s), jnp.stack(kr_s), jnp.stack(ssm_s), jnp.stack(conv_s), jnp.stack(gv_s))
```

```python
import functools
import math

import numpy as np
import jax
import jax.numpy as jnp
from jax import lax
from jax.experimental import pallas as pl
from jax.experimental.pallas import tpu as pltpu

F32 = jnp.float32
BF16 = jnp.bfloat16

D_MODEL = 1024
EPS = 1e-6
NEG_INF = -1e30
MLA_HEADS = 8
Q_LORA = 384
KV_LORA = 256
QK_NOPE = 64
QK_ROPE = 32
V_HEAD = 64
ROPE_THETA = 10000.0
MLA_SCALE = 1.0 / math.sqrt(QK_NOPE + QK_ROPE)
SSM_HEADS = 8
SSM_HEAD_DIM = 64
D_INNER = SSM_HEADS * SSM_HEAD_DIM
SSM_GROUPS = 2
D_STATE = 64
CONV_K = 4
CONV_DIM = D_INNER + 2 * SSM_GROUPS * D_STATE
GM_GROUPS = 8
GM_GROUP_DIM = 64
GM_WIDTH = GM_GROUPS * GM_GROUP_DIM
MEM_HEADS = 4
MEM_HEAD_DIM = 64
MEM_WIDTH = MEM_HEADS * MEM_HEAD_DIM
MEM_SCALE = 1.0 / math.sqrt(MEM_HEAD_DIM)
D_FF = 2816
IN_SIZES = (Q_LORA, KV_LORA, QK_ROPE, D_INNER, CONV_DIM, SSM_HEADS, GM_WIDTH, GM_WIDTH)

LANES = 128
CHUNK = 128
HEAD_PAD = LANES
ROPE_LO = QK_NOPE
ROPE_HALF = QK_ROPE // 2
MISC_DT = 96
VMEM_LIMIT = 56 * 1024 * 1024

P_CQ, P_CKV, P_Z, P_XBC, P_U, P_V, P_MISC, P_END = 0, 384, 640, 1152, 1920, 2432, 2944, 3072

NT_DIMS = (((1,), (1,)), ((), ()))
TN_DIMS = (((0,), (0,)), ((), ()))


def _pick(n, cands):
    for c in cands:
        if n % c == 0:
            return c
    raise ValueError(f"no tile in {cands} divides {n}")


def _rms(x, g):
    return x * lax.rsqrt(jnp.mean(x * x, axis=-1, keepdims=True) + EPS) * g


def _silu(x):
    return x * (1.0 / (1.0 + jnp.exp(-x)))


def _dot(a, b):
    return jnp.dot(a, b, preferred_element_type=F32)


def _dot_nt(a, b):
    return lax.dot_general(a, b, NT_DIMS, preferred_element_type=F32)


def _dot_exact(a, b, dims=None):
    if dims is None:
        return jnp.dot(a, b, precision=lax.Precision.HIGHEST, preferred_element_type=F32)
    return lax.dot_general(a, b, dims, precision=lax.Precision.HIGHEST, preferred_element_type=F32)


def _const_spec(shape):
    nd = len(shape)
    return pl.BlockSpec(shape, lambda *_: (0,) * nd)


def _resident_spec(shape):
    nd = len(shape)
    return pl.BlockSpec(shape, lambda *_: (0,) * nd, pipeline_mode=pl.Buffered(1))


def _params(sem):
    return pltpu.CompilerParams(dimension_semantics=sem, vmem_limit_bytes=VMEM_LIMIT)


FF_SPLIT = 2


def _ffn_body(*refs, has_extra, has_final):
    it = iter(refs)
    x_ref = next(it)
    if has_extra:
        e_ref, we_ref = next(it), next(it)
    g_ref, win_ref, wout_ref = next(it), next(it), next(it)
    if has_final:
        fg_ref = next(it)
    o_ref = next(it)
    if has_final:
        y_ref = next(it)

    x = x_ref[...]
    if has_extra:
        x = x + _dot(e_ref[...].astype(BF16), we_ref[...])
    h = _rms(x, g_ref[...]).astype(BF16)
    cw = D_FF // FF_SPLIT
    acc = jnp.zeros_like(x)
    for c in range(FF_SPLIT):
        gate = _dot(h, win_ref[:, c * cw:(c + 1) * cw])
        up = _dot(h, win_ref[:, D_FF + c * cw:D_FF + (c + 1) * cw])
        act = (_silu(gate) * up).astype(BF16)
        acc = acc + _dot(act, wout_ref[c * cw:(c + 1) * cw, :])
    out = x + 0.5 * acc
    o_ref[...] = out
    if has_final:
        y_ref[...] = _rms(out, fg_ref[...])


def _ffn(x, norm_g, w_in, w_out, extra=None, w_extra=None, final_g=None):
    t = x.shape[0]
    tm = _pick(t, (512, 256, 128, 64, 32, 16, 8))
    has_extra = extra is not None
    has_final = final_g is not None
    row = lambda i: (i, 0)
    args, specs = [x], [pl.BlockSpec((tm, D_MODEL), row)]
    if has_extra:
        args += [extra, w_extra]
        specs += [pl.BlockSpec((tm, extra.shape[1]), row), _resident_spec(w_extra.shape)]
    args += [norm_g.reshape(1, D_MODEL), w_in, w_out]
    specs += [_const_spec((1, D_MODEL)), _resident_spec(w_in.shape), _resident_spec(w_out.shape)]
    out_shape = [jax.ShapeDtypeStruct((t, D_MODEL), F32)]
    out_specs = [pl.BlockSpec((tm, D_MODEL), row)]
    if has_final:
        args.append(final_g.reshape(1, D_MODEL))
        specs.append(_const_spec((1, D_MODEL)))
        out_shape.append(jax.ShapeDtypeStruct((t, D_MODEL), F32))
        out_specs.append(pl.BlockSpec((tm, D_MODEL), row))
    res = pl.pallas_call(
        functools.partial(_ffn_body, has_extra=has_extra, has_final=has_final),
        grid=(t // tm,), in_specs=specs, out_specs=out_specs, out_shape=out_shape,
        input_output_aliases={0: 0}, compiler_params=_params(("parallel",)), name="ffn",
    )(*args)
    return res if has_final else res[0]


def _rope(blk, c, s1, s2):
    return blk * c + pltpu.roll(blk, ROPE_HALF, 1) * s1 + pltpu.roll(blk, LANES - ROPE_HALF, 1) * s2


def _mixin_body(x_ref, g_ref, win_ref, qn_ref, wuq_ref, kvn_ref, wuk_ref, c_ref, s1_ref, s2_ref,
                q_ref, k_ref, ckv_ref, ckvb_ref, krp_ref, misc_ref, z_ref, xbc_ref, u_ref, v_ref):
    h = _rms(x_ref[...], g_ref[...]).astype(BF16)
    p = _dot(h, win_ref[...])
    z_ref[...] = p[:, P_Z:P_XBC]
    xbc_ref[...] = p[:, P_XBC:P_U]
    u_ref[...] = p[:, P_U:P_V]
    v_ref[...] = p[:, P_V:P_MISC]
    misc = p[:, P_MISC:P_END]
    misc_ref[...] = misc
    c, s1, s2 = c_ref[...], s1_ref[...], s2_ref[...]
    krp = _rope(misc, c, s1, s2)
    krp_ref[...] = krp
    ckvn = _rms(p[:, P_CKV:P_Z], kvn_ref[...])
    ckv_ref[...] = ckvn
    ckvb = ckvn.astype(BF16)
    ckvb_ref[...] = ckvb
    kn = _dot(ckvb, wuk_ref[...])
    qn = _rms(p[:, P_CQ:P_CKV], qn_ref[...]).astype(BF16)
    q = _dot(qn, wuq_ref[...]) * MLA_SCALE
    for hh in range(MLA_HEADS):
        sl = slice(hh * HEAD_PAD, (hh + 1) * HEAD_PAD)
        q_ref[hh] = _rope(q[:, sl], c, s1, s2).astype(BF16)
        k_ref[hh] = (kn[:, sl] + krp).astype(BF16)


def _mixin(x, lw, tabs):
    t = x.shape[0]
    tm = _pick(t, (512, 256, 128, 64, 32, 16))
    row = lambda i: (i, 0)
    hrow = lambda i: (0, i, 0)
    wide = MLA_HEADS * HEAD_PAD
    outs = [
        ((MLA_HEADS, t, HEAD_PAD), BF16, pl.BlockSpec((MLA_HEADS, tm, HEAD_PAD), hrow)),
        ((MLA_HEADS, t, HEAD_PAD), BF16, pl.BlockSpec((MLA_HEADS, tm, HEAD_PAD), hrow)),
        ((t, KV_LORA), F32, pl.BlockSpec((tm, KV_LORA), row)),
        ((t, KV_LORA), BF16, pl.BlockSpec((tm, KV_LORA), row)),
        ((t, LANES), F32, pl.BlockSpec((tm, LANES), row)),
        ((t, LANES), F32, pl.BlockSpec((tm, LANES), row)),
        ((t, D_INNER), F32, pl.BlockSpec((tm, D_INNER), row)),
        ((t, CONV_DIM), F32, pl.BlockSpec((tm, CONV_DIM), row)),
        ((t, GM_WIDTH), F32, pl.BlockSpec((tm, GM_WIDTH), row)),
        ((t, GM_WIDTH), F32, pl.BlockSpec((tm, GM_WIDTH), row)),
    ]
    return pl.pallas_call(
        _mixin_body, grid=(t // tm,),
        in_specs=[pl.BlockSpec((tm, D_MODEL), row), _const_spec((1, D_MODEL)),
                  _resident_spec((D_MODEL, P_END)), _const_spec((1, Q_LORA)),
                  _resident_spec((Q_LORA, wide)), _const_spec((1, KV_LORA)),
                  _resident_spec((KV_LORA, wide)),
                  pl.BlockSpec((tm, LANES), row), pl.BlockSpec((tm, LANES), row),
                  pl.BlockSpec((tm, LANES), row)],
        out_specs=[o[2] for o in outs],
        out_shape=[jax.ShapeDtypeStruct(o[0], o[1]) for o in outs],
        compiler_params=_params(("parallel",)), name="mixin",
    )(x, lw["mix_norm"], lw["w_in_p"], lw["q_norm"], lw["w_uq_p"], lw["kv_norm"], lw["w_uk_p"], *tabs)


def _pattn_body(qi_ref, kj_ref, q_ref, k_ref, v_ref, wuv_ref, o_ref, m_ref, l_ref, acc_ref, *, tq, kvb):
    n = pl.program_id(1)
    i = qi_ref[n]
    j = kj_ref[n]
    last = (i * tq + tq - 1) // kvb

    @pl.when(j == 0)
    def _():
        m_ref[...] = jnp.full(m_ref.shape, NEG_INF, F32)
        l_ref[...] = jnp.zeros(l_ref.shape, F32)
        acc_ref[...] = jnp.zeros(acc_ref.shape, F32)

    def step(masked):
        v = v_ref[...]
        if masked:
            qpos = i * tq + lax.broadcasted_iota(jnp.int32, (tq, kvb), 0)
            kpos = j * kvb + lax.broadcasted_iota(jnp.int32, (tq, kvb), 1)
            ok = kpos <= qpos

        def head(hh, carry):
            s = _dot_nt(q_ref[hh], k_ref[hh])
            if masked:
                s = jnp.where(ok, s, NEG_INF)
            m_prev = m_ref[hh]
            m_new = jnp.maximum(m_prev, jnp.max(s, axis=1, keepdims=True))
            alpha = jnp.exp(m_prev - m_new)
            p = jnp.exp(s - m_new[:, :1])
            l_ref[hh] = alpha * l_ref[hh] + jnp.sum(p, axis=1, keepdims=True)
            acc_ref[hh] = acc_ref[hh] * alpha[:, :1] + _dot(p.astype(BF16), v)
            m_ref[hh] = m_new
            return carry

        lax.fori_loop(0, MLA_HEADS, head, 0)

    @pl.when(j == last)
    def _():
        step(True)
        for hh in range(MLA_HEADS):
            o = (acc_ref[hh] / l_ref[hh][:, :1]).astype(BF16)
            o_ref[:, hh * V_HEAD:(hh + 1) * V_HEAD] = _dot(o, wuv_ref[hh]).astype(BF16)

    @pl.when(j != last)
    def _():
        step(False)


def _prompt_attention(q, k, vb, w_uv, bsz, seq):
    tq = _pick(seq, (256, 128))
    kvb = _pick(seq, (512, 256, 128))
    nq, nkv = seq // tq, seq // kvb
    pairs = [(i, j) for i in range(nq) for j in range((i * tq + tq - 1) // kvb + 1)]
    qi = jnp.asarray(np.array([p[0] for p in pairs], np.int32))
    kj = jnp.asarray(np.array([p[1] for p in pairs], np.int32))
    grid_spec = pltpu.PrefetchScalarGridSpec(
        num_scalar_prefetch=2, grid=(bsz, len(pairs)),
        in_specs=[
            pl.BlockSpec((MLA_HEADS, tq, HEAD_PAD), lambda b, n, qi, kj: (0, b * nq + qi[n], 0)),
            pl.BlockSpec((MLA_HEADS, kvb, HEAD_PAD), lambda b, n, qi, kj: (0, b * nkv + kj[n], 0)),
            pl.BlockSpec((kvb, KV_LORA), lambda b, n, qi, kj: (b * nkv + kj[n], 0)),
            pl.BlockSpec((MLA_HEADS, KV_LORA, V_HEAD), lambda b, n, qi, kj: (0, 0, 0)),
        ],
        out_specs=pl.BlockSpec((tq, MLA_HEADS * V_HEAD), lambda b, n, qi, kj: (b * nq + qi[n], 0)),
        scratch_shapes=[pltpu.VMEM((MLA_HEADS, tq, LANES), F32), pltpu.VMEM((MLA_HEADS, tq, LANES), F32),
                        pltpu.VMEM((MLA_HEADS, tq, KV_LORA), F32)],
    )
    return pl.pallas_call(
        functools.partial(_pattn_body, tq=tq, kvb=kvb), grid_spec=grid_spec,
        out_shape=jax.ShapeDtypeStruct((bsz * seq, MLA_HEADS * V_HEAD), BF16),
        compiler_params=_params(("parallel", "arbitrary")), name="prompt_attn",
    )(qi, kj, q, k, vb, w_uv)


def _sattn_body(pt_ref, q_ref, ckvn_ref, krn_ref, wukt_ref, wuv_ref, ckv_hbm, ckr_hbm, o_ref,
                kbuf, rbuf, sem, qlat_ref, qrope_ref, m_ref, l_ref, acc_ref, *, layer, pc, n_seq, n_ch, page):
    b = pl.program_id(0)
    c = pl.program_id(1)
    step = b * n_ch + c
    slot = step % 2
    n_new = q_ref.shape[1] // MLA_HEADS

    def copies(bb, cc, sl):
        out = []
        for pg in range(pc):
            pid = pt_ref[bb, cc * pc + pg]
            out.append(pltpu.make_async_copy(ckv_hbm.at[layer, pid], kbuf.at[sl, pl.ds(pg * page, page)], sem.at[0, sl]))
            out.append(pltpu.make_async_copy(ckr_hbm.at[layer, pid], rbuf.at[sl, pl.ds(pg * page, page)], sem.at[1, sl]))
        return out

    @pl.when(step == 0)
    def _():
        for cp in copies(b, c, slot):
            cp.start()

    @pl.when(step + 1 < n_seq * n_ch)
    def _():
        wrap = c + 1 == n_ch
        nb = jnp.where(wrap, b + 1, b)
        nc = jnp.where(wrap, 0, c + 1)
        for cp in copies(nb, nc, 1 - slot):
            cp.start()

    @pl.when(c == 0)
    def _():
        m_ref[...] = jnp.full(m_ref.shape, NEG_INF, F32)
        l_ref[...] = jnp.zeros(l_ref.shape, F32)
        acc_ref[...] = jnp.zeros(acc_ref.shape, F32)
        q32 = q_ref[0].astype(F32)
        for hh in range(MLA_HEADS):
            qh = q32[hh * n_new:(hh + 1) * n_new, :].astype(BF16)
            qlat_ref[hh * n_new:(hh + 1) * n_new, :] = _dot(qh, wukt_ref[hh])
        qrope_ref[...] = q32[:, ROPE_LO:ROPE_LO + QK_ROPE]

    def update(s, vals):
        m_prev = m_ref[...]
        m_new = jnp.maximum(m_prev, jnp.max(s, axis=1, keepdims=True))
        alpha = jnp.exp(m_prev - m_new)
        p = jnp.exp(s - m_new[:, :1])
        l_ref[...] = alpha * l_ref[...] + jnp.sum(p, axis=1, keepdims=True)
        acc_ref[...] = acc_ref[...] * alpha[:, :1] + _dot(p.astype(BF16), vals)
        m_ref[...] = m_new

    for cp in copies(b, c, slot):
        cp.wait()
    qlat = qlat_ref[...].astype(BF16)
    qrope = qrope_ref[...].astype(BF16)
    kc = kbuf[slot].astype(BF16)
    rc = rbuf[slot].astype(BF16)
    update(_dot_nt(qlat, kc) + _dot_nt(qrope, rc), kc)

    @pl.when(c == n_ch - 1)
    def _():
        pad = LANES - n_new
        kn = jnp.concatenate([ckvn_ref[...], jnp.zeros((pad, KV_LORA), F32)], axis=0).astype(BF16)
        rn = jnp.concatenate([krn_ref[...][:, ROPE_LO:ROPE_LO + QK_ROPE], jnp.zeros((pad, QK_ROPE), F32)],
                             axis=0).astype(BF16)
        s = _dot_nt(qlat, kn) + _dot_nt(qrope, rn)
        tpos = lax.broadcasted_iota(jnp.int32, s.shape, 0) % n_new
        kpos = lax.broadcasted_iota(jnp.int32, s.shape, 1)
        update(jnp.where(kpos <= tpos, s, NEG_INF), kn)
        o = (acc_ref[...] / l_ref[...][:, :1])
        pieces = [_dot(o[hh * n_new:(hh + 1) * n_new, :].astype(BF16), wuv_ref[hh]) for hh in range(MLA_HEADS)]
        o_ref[0] = jnp.concatenate(pieces, axis=1).astype(BF16)


def _sample_attention(q_s, ckv_all, krp_all, w_ukt_p, w_uv, cache_kv, cache_kr, page_table, layer, tp, n_new):
    n_seq, n_pages = page_table.shape
    page = cache_kv.shape[2]
    pc = _pick(n_pages, (8, 4, 2, 1))
    n_ch = n_pages // pc
    rows = MLA_HEADS * n_new
    off = tp // n_new
    grid_spec = pltpu.PrefetchScalarGridSpec(
        num_scalar_prefetch=1, grid=(n_seq, n_ch),
        in_specs=[
            pl.BlockSpec((1, rows, HEAD_PAD), lambda b, c, pt: (b, 0, 0)),
            pl.BlockSpec((n_new, KV_LORA), lambda b, c, pt: (off + b, 0)),
            pl.BlockSpec((n_new, LANES), lambda b, c, pt: (off + b, 0)),
            pl.BlockSpec((MLA_HEADS, HEAD_PAD, KV_LORA), lambda b, c, pt: (0, 0, 0)),
            pl.BlockSpec((MLA_HEADS, KV_LORA, V_HEAD), lambda b, c, pt: (0, 0, 0)),
            pl.BlockSpec(memory_space=pl.ANY),
            pl.BlockSpec(memory_space=pl.ANY),
        ],
        out_specs=pl.BlockSpec((1, n_new, MLA_HEADS * V_HEAD), lambda b, c, pt: (b, 0, 0)),
        scratch_shapes=[
            pltpu.VMEM((2, pc * page, KV_LORA), F32), pltpu.VMEM((2, pc * page, QK_ROPE), F32),
            pltpu.SemaphoreType.DMA((2, 2)),
            pltpu.VMEM((rows, KV_LORA), F32), pltpu.VMEM((rows, QK_ROPE), F32),
            pltpu.VMEM((rows, LANES), F32), pltpu.VMEM((rows, LANES), F32), pltpu.VMEM((rows, KV_LORA), F32),
        ],
    )
    out = pl.pallas_call(
        functools.partial(_sattn_body, layer=layer, pc=pc, n_seq=n_seq, n_ch=n_ch, page=page),
        grid_spec=grid_spec,
        out_shape=jax.ShapeDtypeStruct((n_seq, n_new, MLA_HEADS * V_HEAD), BF16),
        compiler_params=_params(("arbitrary", "arbitrary")), name="sample_attn",
    )(page_table, q_s, ckv_all, krp_all, w_ukt_p, w_uv, cache_kv, cache_kr)
    return out.reshape(n_seq * n_new, MLA_HEADS * V_HEAD)


def _ssd_consts(seg):
    r = np.arange(CHUNK)
    same = (r[:, None] // seg) == (r[None, :] // seg)
    caus = (same & (r[None, :] <= r[:, None])).astype(np.float32)
    plast = (r[None, :] == (r[:, None] // seg) * seg + seg - 1).astype(np.float32)
    et = np.zeros((D_INNER, LANES), np.float32)
    et[np.arange(D_INNER), MISC_DT + np.arange(D_INNER) // SSM_HEAD_DIM] = 1.0
    return jnp.asarray(caus), jnp.asarray(plast), jnp.asarray(et)


def _ssd_tile(act, z, misc, caus, plast, et, dtb, alog, dskip, ng, h_in_fn, seg):
    nseg = CHUNK // seg
    lane = lax.broadcasted_iota(jnp.int32, (1, LANES), 1)
    hmask = (lane >= MISC_DT) & (lane < MISC_DT + SSM_HEADS)
    lo = lax.broadcasted_iota(jnp.int32, (CHUNK, LANES), 1) < SSM_HEAD_DIM
    xs = act[:, :D_INNER]
    bm = act[:, D_INNER:D_INNER + LANES]
    cm = act[:, D_INNER + LANES:D_INNER + 2 * LANES]
    dtv = jnp.where(hmask, jax.nn.softplus(misc + dtb), 0.0)
    a_row = jnp.where(hmask, -jnp.exp(alog), 0.0)
    acs = _dot_exact(caus, dtv * a_row)
    acs_last = _dot_exact(plast, acs)
    e_acs = jnp.exp(acs)
    to_end = jnp.exp(acs_last - acs)
    e_last = jnp.exp(acs_last)
    acs_t = acs.T
    dec_all = _dot_exact(et, e_last, NT_DIMS)

    bm_b = bm.astype(BF16)
    cm_g = [jnp.where(lo, cm, 0.0).astype(BF16), jnp.where(lo, 0.0, cm).astype(BF16)]
    g_mat = [_dot_nt(cm_g[g], bm_b) for g in range(SSM_GROUPS)]
    cm_half = [cm[:, :D_STATE].astype(BF16), cm[:, D_STATE:].astype(BF16)]

    def col(arr, hh):
        return arr[:, MISC_DT + hh:MISC_DT + hh + 1]

    def pair(arr, pb):
        return jnp.where(lo, col(arr, 2 * pb), col(arr, 2 * pb + 1))

    hpg = SSM_HEADS // SSM_GROUPS
    rows_g = hpg * SSM_HEAD_DIM
    y_off = []
    for g in range(SSM_GROUPS):
        parts = []
        for j in range(nseg):
            hj = h_in_fn(j)[g * rows_g:(g + 1) * rows_g, :].astype(BF16)
            parts.append(_dot_nt(cm_half[g][j * seg:(j + 1) * seg, :], hj))
        y_off.append(parts[0] if nseg == 1 else jnp.concatenate(parts, axis=0))

    y_blocks, xte_blocks = [], []
    for pb in range(SSM_HEADS // 2):
        g = (2 * pb) // hpg
        sl = slice(pb * LANES, (pb + 1) * LANES)
        xs_p = xs[:, sl]
        xdt = xs_p * pair(dtv, pb)
        xte_blocks.append((xdt * pair(to_end, pb)).astype(BF16))
        xdt_b = xdt.astype(BF16)
        yd = []
        for hh in (2 * pb, 2 * pb + 1):
            segm = col(acs, hh) - acs_t[MISC_DT + hh:MISC_DT + hh + 1, :]
            decay = jnp.exp(jnp.where(caus > 0.0, segm, NEG_INF))
            yd.append(_dot((g_mat[g] * decay).astype(BF16), xdt_b))
        y_d = jnp.where(lo, yd[0], yd[1])
        off_sl = slice((pb % (hpg // 2)) * LANES, (pb % (hpg // 2) + 1) * LANES)
        y_o = y_off[g][:, off_sl] * pair(e_acs, pb)
        y_blocks.append(y_d + y_o + dskip[:, sl] * xs_p)
    y = jnp.concatenate(y_blocks, axis=1)
    yn = _rms(y * _silu(z), ng)

    xte = jnp.concatenate(xte_blocks, axis=1)
    rowi = lax.broadcasted_iota(jnp.int32, (D_INNER, D_STATE), 0) < rows_g
    new_states = []
    if nseg == 1:
        s_all = lax.dot_general(xte, bm_b, TN_DIMS, preferred_element_type=F32)
        s_sel = jnp.where(rowi, s_all[:, :D_STATE], s_all[:, D_STATE:])
        new_states.append(h_in_fn(0) * dec_all[:, 0:1] + s_sel)
    else:
        tr = lax.broadcasted_iota(jnp.int32, (CHUNK, LANES), 0) // seg
        bbd = jnp.concatenate([jnp.where(tr == j, bm, 0.0) for j in range(nseg)], axis=1).astype(BF16)
        s_all = lax.dot_general(xte, bbd, TN_DIMS, preferred_element_type=F32)
        for j in range(nseg):
            sj = s_all[:, j * LANES:(j + 1) * LANES]
            s_sel = jnp.where(rowi, sj[:, :D_STATE], sj[:, D_STATE:])
            new_states.append(h_in_fn(j) * dec_all[:, j * seg:j * seg + 1] + s_sel)
    return yn, new_states


def _ssd_prompt_body(z_ref, xbc_ref, misc_ref, caus_ref, plast_ref, et_ref, cw_ref, cb_ref, dtb_ref, alog_ref,
                     dskip_ref, ng_ref, y_ref, st_ref, xpad_ref, h_ref):
    c = pl.program_id(1)
    hist = CONV_K - 1

    @pl.when(c == 0)
    def _():
        xpad_ref[0:8, :] = jnp.zeros((8, CONV_DIM), F32)
        h_ref[...] = jnp.zeros(h_ref.shape, F32)

    x = xbc_ref[...]
    xpad_ref[8:8 + CHUNK, :] = x
    conv = cb_ref[...] + x * cw_ref[hist:hist + 1, :]
    for k in range(hist):
        conv = conv + xpad_ref[8 - hist + k:8 - hist + k + CHUNK, :] * cw_ref[k:k + 1, :]
    xpad_ref[0:8, :] = x[CHUNK - 8:, :]
    act = _silu(conv)
    yn, new = _ssd_tile(act, z_ref[...], misc_ref[...], caus_ref[...], plast_ref[...], et_ref[...],
                        dtb_ref[...], alog_ref[...], dskip_ref[...], ng_ref[...], lambda j: h_ref[...], CHUNK)
    y_ref[...] = yn.astype(BF16)
    h_ref[...] = new[0]

    @pl.when(c == pl.num_programs(1) - 1)
    def _():
        st_ref[0] = new[0]


def _ssd_sample_body(z_ref, xbc_ref, misc_ref, sp_ref, h0_ref, caus_ref, plast_ref, et_ref, cw_ref, cb_ref,
                     dtb_ref, alog_ref, dskip_ref, ng_ref, y_ref, st_ref, *, seg):
    hist = CONV_K - 1
    x = xbc_ref[...]
    sp = sp_ref[...]
    tpos = lax.broadcasted_iota(jnp.int32, (CHUNK, CONV_DIM), 0) % seg
    conv = cb_ref[...] + x * cw_ref[hist:hist + 1, :]
    for j in range(1, CONV_K):
        xin = pltpu.roll(x, j, 0)
        up = hist - j
        sin = sp if up == 0 else pltpu.roll(sp, CHUNK - up, 0)
        conv = conv + jnp.where(tpos >= j, xin, sin) * cw_ref[hist - j:hist - j + 1, :]
    act = _silu(conv)
    yn, new = _ssd_tile(act, z_ref[...], misc_ref[...], caus_ref[...], plast_ref[...], et_ref[...],
                        dtb_ref[...], alog_ref[...], dskip_ref[...], ng_ref[...], lambda j: h0_ref[j], seg)
    y_ref[...] = yn.astype(BF16)
    for j in range(len(new)):
        st_ref[j] = new[j]


def _ssd_param_args(lw):
    return [lw["conv_w"], lw["conv_b"], lw["dtb_row"], lw["alog_row"], lw["dskip_exp"], lw["ssm_norm"]]


def _ssd_param_specs():
    return [_const_spec((CONV_K, CONV_DIM)), _const_spec((1, CONV_DIM)), _const_spec((1, LANES)),
            _const_spec((1, LANES)), _const_spec((1, D_INNER)), _const_spec((1, D_INNER))]


def _ssd_prompt(z, xbc, misc, lw, bsz, seq):
    nc = seq // CHUNK
    caus, plast, et = _ssd_consts(CHUNK)
    row = lambda b, c: (b * nc + c, 0)
    y, st = pl.pallas_call(
        _ssd_prompt_body, grid=(bsz, nc),
        in_specs=[pl.BlockSpec((CHUNK, D_INNER), row), pl.BlockSpec((CHUNK, CONV_DIM), row),
                  pl.BlockSpec((CHUNK, LANES), row), _const_spec((CHUNK, CHUNK)), _const_spec((CHUNK, CHUNK)),
                  _const_spec((D_INNER, LANES))] + _ssd_param_specs(),
        out_specs=[pl.BlockSpec((CHUNK, D_INNER), row),
                   pl.BlockSpec((1, D_INNER, D_STATE), lambda b, c: (b, 0, 0))],
        out_shape=[jax.ShapeDtypeStruct((bsz * seq, D_INNER), BF16),
                   jax.ShapeDtypeStruct((bsz, D_INNER, D_STATE), F32)],
        scratch_shapes=[pltpu.VMEM((8 + CHUNK, CONV_DIM), F32), pltpu.VMEM((D_INNER, D_STATE), F32)],
        compiler_params=_params(("parallel", "arbitrary")), name="ssd_prompt",
    )(z, xbc, misc, caus, plast, et, *_ssd_param_args(lw))
    return y, st


def _ssd_sample(z, xbc, misc, sp, h0, lw, tp, n_seq, seg):
    nseg = CHUNK // seg
    nt = n_seq // nseg
    off = tp // CHUNK
    caus, plast, et = _ssd_consts(seg)
    row = lambda i: (off + i, 0)
    y, st = pl.pallas_call(
        functools.partial(_ssd_sample_body, seg=seg), grid=(nt,),
        in_specs=[pl.BlockSpec((CHUNK, D_INNER), row), pl.BlockSpec((CHUNK, CONV_DIM), row),
                  pl.BlockSpec((CHUNK, LANES), row), pl.BlockSpec((CHUNK, CONV_DIM), lambda i: (i, 0)),
                  pl.BlockSpec((nseg, D_INNER, D_STATE), lambda i: (i, 0, 0)),
                  _const_spec((CHUNK, CHUNK)), _const_spec((CHUNK, CHUNK)),
                  _const_spec((D_INNER, LANES))] + _ssd_param_specs(),
        out_specs=[pl.BlockSpec((CHUNK, D_INNER), lambda i: (i, 0)),
                   pl.BlockSpec((nseg, D_INNER, D_STATE), lambda i: (i, 0, 0))],
        out_shape=[jax.ShapeDtypeStruct((n_seq * seg, D_INNER), BF16),
                   jax.ShapeDtypeStruct((n_seq, D_INNER, D_STATE), F32)],
        compiler_params=_params(("parallel",)), name="ssd_sample",
    )(z, xbc, misc, sp, h0, caus, plast, et, *_ssd_param_args(lw))
    return y, st


def _gmlp_body(u_ref, v_ref, w_ref, mask_ref, b_ref, ng_ref, o_ref, *vn_out, reps):
    lo = lax.broadcasted_iota(jnp.int32, (CHUNK, LANES), 1) < GM_GROUP_DIM
    mask = mask_ref[...] > 0.0
    wm = [jnp.where(mask, w_ref[g], 0.0).astype(BF16) for g in range(GM_GROUPS)]
    bias = b_ref[...]
    for r in range(reps):
        rs = slice(r * CHUNK, (r + 1) * CHUNK)
        ug = jax.nn.gelu(u_ref[rs, :])
        vn = _rms(jax.nn.gelu(v_ref[rs, :]), ng_ref[...])
        if vn_out:
            vn_out[0][rs, :] = vn
        vb = vn.astype(BF16)
        for pb in range(GM_GROUPS // 2):
            sl = slice(pb * LANES, (pb + 1) * LANES)
            s = jnp.where(lo, _dot(wm[2 * pb], vb[:, sl]), _dot(wm[2 * pb + 1], vb[:, sl])) + bias[:, sl]
            o_ref[rs, sl] = (ug[:, sl] * s).astype(BF16)


def _gmlp(u, v, w_sp, mask, bias_exp, norm_g, row_off, n_rows, want_vn):
    reps = _pick(n_rows // CHUNK, (4, 2, 1))
    tm = reps * CHUNK
    off = row_off // tm
    row = lambda i: (off + i, 0)
    out_shape = [jax.ShapeDtypeStruct((n_rows, GM_WIDTH), BF16)]
    out_specs = [pl.BlockSpec((tm, GM_WIDTH), lambda i: (i, 0))]
    if want_vn:
        out_shape.append(jax.ShapeDtypeStruct((n_rows, GM_WIDTH), F32))
        out_specs.append(pl.BlockSpec((tm, GM_WIDTH), lambda i: (i, 0)))
    return pl.pallas_call(
        functools.partial(_gmlp_body, reps=reps), grid=(n_rows // tm,),
        in_specs=[pl.BlockSpec((tm, GM_WIDTH), row), pl.BlockSpec((tm, GM_WIDTH), row),
                  _const_spec((GM_GROUPS, CHUNK, CHUNK)), _const_spec((CHUNK, CHUNK)),
                  _const_spec((CHUNK, GM_WIDTH)), _const_spec((1, GM_WIDTH))],
        out_specs=out_specs, out_shape=out_shape,
        compiler_params=_params(("parallel",)), name="gmlp",
    )(u, v, w_sp, mask, bias_exp, norm_g)


def _merge_body(x_ref, oh_ref, ys_ref, gm_ref, ng_ref, wg_ref, bg_ref, womla_ref, wossm_ref, wogm_ref,
                wout_ref, mqn_ref, wmq_ref, xo_ref, qm_ref):
    x = x_ref[...]
    h = _rms(x, ng_ref[...]).astype(BF16)
    gates = _dot(h, wg_ref[...]) + bg_ref[...]
    gates = 1.0 / (1.0 + jnp.exp(-gates))
    mix = (gates[:, :D_MODEL] * _dot(oh_ref[...], womla_ref[...])
           + gates[:, D_MODEL:2 * D_MODEL] * _dot(ys_ref[...], wossm_ref[...])
           + gates[:, 2 * D_MODEL:] * _dot(gm_ref[...], wogm_ref[...]))
    x2 = x + _dot(mix.astype(BF16), wout_ref[...])
    xo_ref[...] = x2
    hq = _rms(x2, mqn_ref[...]).astype(BF16)
    qm_ref[...] = _dot(hq, wmq_ref[...])


def _merge(x, oh, ys, gm, lw):
    t = x.shape[0]
    tm = _pick(t, (512, 256, 128, 64, 32, 16))
    row = lambda i: (i, 0)
    half = MLA_HEADS * V_HEAD
    return pl.pallas_call(
        _merge_body, grid=(t // tm,),
        in_specs=[pl.BlockSpec((tm, D_MODEL), row), pl.BlockSpec((tm, half), row),
                  pl.BlockSpec((tm, D_INNER), row), pl.BlockSpec((tm, GM_WIDTH), row),
                  _const_spec((1, D_MODEL)), _resident_spec((D_MODEL, 3 * D_MODEL)), _const_spec((1, 3 * D_MODEL)),
                  _resident_spec((half, D_MODEL)), _resident_spec((D_INNER, D_MODEL)),
                  _resident_spec((GM_WIDTH, D_MODEL)), _resident_spec((D_MODEL, D_MODEL)),
                  _const_spec((1, D_MODEL)), _resident_spec((D_MODEL, MEM_WIDTH))],
        out_specs=[pl.BlockSpec((tm, D_MODEL), row), pl.BlockSpec((tm, MEM_WIDTH), row)],
        out_shape=[jax.ShapeDtypeStruct((t, D_MODEL), F32), jax.ShapeDtypeStruct((t, MEM_WIDTH), F32)],
        input_output_aliases={0: 0}, compiler_params=_params(("parallel",)), name="merge",
    )(x, oh, ys, gm, lw["mix_norm"], lw["w_gate"], lw["b_gate"], lw["w_o_mla"], lw["w_o_ssm"], lw["w_o_gm"],
      lw["w_out"], lw["mem_q_norm"], lw["w_mem_q"])


def _memattn_body(q_ref, k_ref, v_ref, o_ref):
    q = q_ref[...]
    kb = k_ref[0].astype(BF16)
    vb = v_ref[0].astype(BF16)
    lane = lax.broadcasted_iota(jnp.int32, q.shape, 1) // MEM_HEAD_DIM
    out = jnp.zeros(q.shape, F32)
    for hh in range(MEM_HEADS):
        sel = lane == hh
        s = _dot_nt(jnp.where(sel, q, 0.0).astype(BF16), kb) * MEM_SCALE
        p = jnp.exp(s - jnp.max(s, axis=1, keepdims=True))
        p = p / jnp.sum(p, axis=1, keepdims=True)
        out = out + jnp.where(sel, _dot(p.astype(BF16), vb), 0.0)
    o_ref[...] = out


def _mem_attention(qm, mem_k, mem_v, row_off, nb, rows_per_b):
    tm = _pick(rows_per_b, (512, 256, 128, 64, 32, 16, 8))
    nt = rows_per_b // tm
    off = row_off // tm
    m = mem_k.shape[1]
    return pl.pallas_call(
        _memattn_body, grid=(nb, nt),
        in_specs=[pl.BlockSpec((tm, MEM_WIDTH), lambda b, i: (off + b * nt + i, 0)),
                  pl.BlockSpec((1, m, MEM_WIDTH), lambda b, i: (b, 0, 0)),
                  pl.BlockSpec((1, m, MEM_WIDTH), lambda b, i: (b, 0, 0))],
        out_specs=pl.BlockSpec((tm, MEM_WIDTH), lambda b, i: (b * nt + i, 0)),
        out_shape=jax.ShapeDtypeStruct((nb * rows_per_b, MEM_WIDTH), F32),
        compiler_params=_params(("parallel", "parallel")), name="mem_attn",
    )(qm, mem_k, mem_v)


def _memkv_body(m_ref, g_ref, w_ref, o_ref):
    o_ref[...] = _dot(_rms(m_ref[...], g_ref[...]).astype(BF16), w_ref[...])


def _memory_kv(mem2d, norm_g, w_kv):
    t = mem2d.shape[0]
    tm = _pick(t, (256, 128, 64, 32, 16, 8))
    return pl.pallas_call(
        _memkv_body, grid=(t // tm,),
        in_specs=[pl.BlockSpec((tm, D_MODEL), lambda i: (i, 0)), _const_spec((1, D_MODEL)),
                  _const_spec((D_MODEL, 2 * MEM_WIDTH))],
        out_specs=pl.BlockSpec((tm, 2 * MEM_WIDTH), lambda i: (i, 0)),
        out_shape=jax.ShapeDtypeStruct((t, 2 * MEM_WIDTH), F32),
        compiler_params=_params(("parallel",)), name="mem_kv",
    )(mem2d, norm_g, w_kv)


def _layer_weights(l, p, seg):
    w_in = p["w_in"][l]
    offs = np.cumsum((0,) + IN_SIZES)
    cq, ckv, kr, z, xbc, dt, u, v = [w_in[:, offs[i]:offs[i + 1]] for i in range(len(IN_SIZES))]
    zc = lambda n: jnp.zeros((D_MODEL, n), F32)
    w_in_p = jnp.concatenate(
        [cq, ckv, z, xbc, u, v, zc(ROPE_LO), kr, dt, zc(LANES - ROPE_LO - QK_ROPE - SSM_HEADS)], axis=1)
    hd = QK_NOPE + QK_ROPE
    w_uq_p = jnp.pad(p["w_uq"][l].reshape(Q_LORA, MLA_HEADS, hd), ((0, 0), (0, 0), (0, HEAD_PAD - hd)))
    w_uk = p["w_uk"][l]
    w_uk_p = jnp.pad(jnp.transpose(w_uk, (1, 0, 2)), ((0, 0), (0, 0), (0, HEAD_PAD - QK_NOPE)))
    w_ukt_p = jnp.pad(jnp.transpose(w_uk, (0, 2, 1)), ((0, 0), (0, HEAD_PAD - QK_NOPE), (0, 0)))
    lane_pad = lambda a: jnp.pad(a.reshape(1, SSM_HEADS), ((0, 0), (MISC_DT, LANES - MISC_DT - SSM_HEADS)))
    w_sp = p["w_spatial"][l]
    b_sp = p["b_spatial"][l]
    bias_p = jnp.repeat(b_sp.T, GM_GROUP_DIM, axis=1)
    reps = CHUNK // seg
    w_sp_s = jnp.tile(w_sp[:, :seg, :seg], (1, reps, reps))
    bias_s = jnp.tile(jnp.repeat(b_sp[:, :seg].T, GM_GROUP_DIM, axis=1), (reps, 1))
    row = lambda a: a.reshape(1, -1)
    return {
        "ffn_norm": p["ffn_norm"][l], "w_ffn_in": p["w_ffn_in"][l].astype(BF16),
        "w_ffn_out": p["w_ffn_out"][l].astype(BF16),
        "mix_norm": row(p["mix_norm"][l]), "w_in_p": w_in_p.astype(BF16),
        "q_norm": row(p["q_norm"][l]), "w_uq_p": w_uq_p.reshape(Q_LORA, MLA_HEADS * HEAD_PAD).astype(BF16),
        "kv_norm": row(p["kv_norm"][l]), "w_uk_p": w_uk_p.reshape(KV_LORA, MLA_HEADS * HEAD_PAD).astype(BF16),
        "w_ukt_p": w_ukt_p.astype(BF16), "w_uv": p["w_uv"][l].astype(BF16),
        "w_o_mla": p["w_o_mla"][l].astype(BF16),
        "conv_w": p["conv_w"][l], "conv_b": row(p["conv_b"][l]),
        "dtb_row": lane_pad(p["dt_bias"][l]), "alog_row": lane_pad(p["a_log"][l]),
        "dskip_exp": row(jnp.repeat(p["d_skip"][l], SSM_HEAD_DIM)), "ssm_norm": row(p["ssm_norm"][l]),
        "w_o_ssm": p["w_o_ssm"][l].astype(BF16),
        "gm_norm": row(p["gm_norm"][l]), "w_sp_p": w_sp, "bias_p": bias_p, "w_sp_s": w_sp_s, "bias_s": bias_s,
        "w_o_gm": p["w_o_gm"][l].astype(BF16),
        "w_gate": p["w_gate"][l].astype(BF16), "b_gate": row(p["b_gate"][l]), "w_out": p["w_out"][l].astype(BF16),
        "mem_q_norm": row(p["mem_q_norm"][l]), "mem_kv_norm": row(p["mem_kv_norm"][l]),
        "w_mem_q": p["w_mem_q"][l].astype(BF16), "w_mem_kv": p["w_mem_kv"][l].astype(BF16),
        "w_mem_o": p["w_mem_o"][l].astype(BF16),
    }


def _rope_tables(pos):
    inv = ROPE_THETA ** (-jnp.arange(ROPE_HALF, dtype=F32) / ROPE_HALF)
    ang = pos.astype(F32)[:, None] * inv[None, :]
    cos, sin = jnp.cos(ang), jnp.sin(ang)
    n = pos.shape[0]
    zero = lambda w: jnp.zeros((n, w), F32)
    tail = LANES - ROPE_LO - QK_ROPE
    c = jnp.concatenate([jnp.ones((n, ROPE_LO), F32), cos, cos, zero(tail)], axis=1)
    s1 = jnp.concatenate([zero(ROPE_LO + ROPE_HALF), sin, zero(tail)], axis=1)
    s2 = jnp.concatenate([zero(ROPE_LO), -sin, zero(ROPE_HALF + tail)], axis=1)
    return c, s1, s2


def _spatial_masks(seg):
    r = np.arange(CHUNK)
    same = (r[:, None] // seg) == (r[None, :] // seg)
    return jnp.asarray((same & (r[None, :] <= r[:, None])).astype(np.float32))


def kernel(x_prompt, x_sample, mem_prompt, cache_kv_latent, cache_k_rope, state_ssm, state_conv, cache_mem_k, cache_mem_v, page_table, ffn_norm, w_ffn_in, w_ffn_out, mix_norm, w_in, q_norm, w_uq, kv_norm, w_uk, w_uv, w_o_mla, conv_w, conv_b, dt_bias, a_log, d_skip, ssm_norm, w_o_ssm, gm_norm, w_spatial, b_spatial, w_o_gm, w_gate, b_gate, w_out, mem_q_norm, mem_kv_norm, w_mem_q, w_mem_kv, w_mem_o, final_norm):
    params = dict(ffn_norm=ffn_norm, w_ffn_in=w_ffn_in, w_ffn_out=w_ffn_out, mix_norm=mix_norm, w_in=w_in,
                  q_norm=q_norm, w_uq=w_uq, kv_norm=kv_norm, w_uk=w_uk, w_uv=w_uv, w_o_mla=w_o_mla,
                  conv_w=conv_w, conv_b=conv_b, dt_bias=dt_bias, a_log=a_log, d_skip=d_skip, ssm_norm=ssm_norm,
                  w_o_ssm=w_o_ssm, gm_norm=gm_norm, w_spatial=w_spatial, b_spatial=b_spatial, w_o_gm=w_o_gm,
                  w_gate=w_gate, b_gate=b_gate, w_out=w_out, mem_q_norm=mem_q_norm, mem_kv_norm=mem_kv_norm,
                  w_mem_q=w_mem_q, w_mem_kv=w_mem_kv, w_mem_o=w_mem_o)
    bsz, seq, _ = x_prompt.shape
    n_seq, n_new, _ = x_sample.shape
    depth = w_in.shape[0]
    past = page_table.shape[1] * cache_kv_latent.shape[2]
    tp, ts = bsz * seq, n_seq * n_new
    mem_tokens = mem_prompt.shape[1]
    assert seq % CHUNK == 0 and CHUNK % n_new == 0 and ts % CHUNK == 0 and n_new % 8 == 0

    pos = jnp.concatenate([jnp.tile(jnp.arange(seq, dtype=jnp.int32), bsz),
                           jnp.tile(past + jnp.arange(n_new, dtype=jnp.int32), n_seq)])
    tabs = _rope_tables(pos)
    mask_p, mask_s = _spatial_masks(CHUNK), _spatial_masks(n_new)
    hist = CONV_K - 1

    x = jnp.concatenate([x_prompt.reshape(tp, D_MODEL), x_sample.reshape(ts, D_MODEL)], axis=0)
    mem2d = mem_prompt.reshape(bsz * mem_tokens, D_MODEL)
    outs = {k: [] for k in ("kvl_p", "kr_p", "ssm_p", "conv_p", "mk_p", "mv_p", "kvl_s", "kr_s", "ssm_s", "conv_s", "gv_s")}
    lws = [_layer_weights(l, params, n_new) for l in range(depth)]
    x = _ffn(x, lws[0]["ffn_norm"][0], lws[0]["w_ffn_in"][0], lws[0]["w_ffn_out"][0])
    y = None
    for l in range(depth):
        lw = lws[l]
        q, k, ckv, ckvb, krp, misc, z, xbc, u, v = _mixin(x, lw, tabs)

        oh_p = _prompt_attention(q, k, ckvb, lw["w_uv"], bsz, seq)
        q_s = jnp.transpose(q[:, tp:, :].reshape(MLA_HEADS, n_seq, n_new, HEAD_PAD), (1, 0, 2, 3))
        oh_s = _sample_attention(q_s.reshape(n_seq, MLA_HEADS * n_new, HEAD_PAD), ckv, krp, lw["w_ukt_p"],
                                 lw["w_uv"], cache_kv_latent, cache_k_rope, page_table, l, tp, n_new)
        oh = jnp.concatenate([oh_p, oh_s], axis=0)

        ys_p, st_p = _ssd_prompt(z, xbc, misc, lw, bsz, seq)
        sp = jnp.pad(state_conv[l], ((0, 0), (0, n_new - hist), (0, 0))).reshape(ts, CONV_DIM)
        h0 = state_ssm[l].reshape(n_seq, D_INNER, D_STATE)
        ys_s, st_s = _ssd_sample(z, xbc, misc, sp, h0, lw, tp, n_seq, n_new)
        ys = jnp.concatenate([ys_p, ys_s], axis=0)

        (gm_p,) = _gmlp(u, v, lw["w_sp_p"], mask_p, lw["bias_p"], lw["gm_norm"], 0, tp, False)
        gm_s, vn_s = _gmlp(u, v, lw["w_sp_s"], mask_s, lw["bias_s"], lw["gm_norm"], tp, ts, True)
        gm = jnp.concatenate([gm_p, gm_s], axis=0)

        x, qm = _merge(x, oh, ys, gm, lw)

        kv = _memory_kv(mem2d, lw["mem_kv_norm"], lw["w_mem_kv"])
        mk_p = kv[:, :MEM_WIDTH].reshape(bsz, mem_tokens, MEM_WIDTH)
        mv_p = kv[:, MEM_WIDTH:].reshape(bsz, mem_tokens, MEM_WIDTH)
        om_p = _mem_attention(qm, mk_p, mv_p, 0, bsz, seq)
        om_s = _mem_attention(qm, cache_mem_k[l].reshape(n_seq, -1, MEM_WIDTH),
                              cache_mem_v[l].reshape(n_seq, -1, MEM_WIDTH), tp, n_seq, n_new)
        om = jnp.concatenate([om_p, om_s], axis=0)

        if l == depth - 1:
            x, y = _ffn(x, lw["ffn_norm"][1], lw["w_ffn_in"][1], lw["w_ffn_out"][1], extra=om,
                        w_extra=lw["w_mem_o"], final_g=final_norm)
        else:
            x = _ffn(x, lw["ffn_norm"][1], lw["w_ffn_in"][1], lw["w_ffn_out"][1], extra=om, w_extra=lw["w_mem_o"])
            nw = lws[l + 1]
            x = _ffn(x, nw["ffn_norm"][0], nw["w_ffn_in"][0], nw["w_ffn_out"][0])

        outs["kvl_p"].append(ckv[:tp].reshape(bsz, seq, KV_LORA))
        outs["kr_p"].append(krp[:tp, ROPE_LO:ROPE_LO + QK_ROPE].reshape(bsz, seq, QK_ROPE))
        outs["ssm_p"].append(st_p.reshape(bsz, SSM_HEADS, SSM_HEAD_DIM, D_STATE))
        outs["conv_p"].append(xbc[:tp].reshape(bsz, seq, CONV_DIM)[:, seq - hist:])
        outs["mk_p"].append(mk_p.reshape(bsz, mem_tokens, MEM_HEADS, MEM_HEAD_DIM))
        outs["mv_p"].append(mv_p.reshape(bsz, mem_tokens, MEM_HEADS, MEM_HEAD_DIM))
        outs["kvl_s"].append(ckv[tp:].reshape(n_seq, n_new, KV_LORA))
        outs["kr_s"].append(krp[tp:, ROPE_LO:ROPE_LO + QK_ROPE].reshape(n_seq, n_new, QK_ROPE))
        outs["ssm_s"].append(st_s.reshape(n_seq, SSM_HEADS, SSM_HEAD_DIM, D_STATE))
        outs["conv_s"].append(xbc[tp:].reshape(n_seq, n_new, CONV_DIM)[:, n_new - hist:])
        outs["gv_s"].append(vn_s.reshape(n_seq, n_new, GM_WIDTH))

    st = {k_: jnp.stack(v_) for k_, v_ in outs.items()}
    return (y[:tp].reshape(bsz, seq, D_MODEL), y[tp:].reshape(n_seq, n_new, D_MODEL),
            st["kvl_p"], st["kr_p"], st["ssm_p"], st["conv_p"], st["mk_p"], st["mv_p"],
            st["kvl_s"], st["kr_s"], st["ssm_s"], st["conv_s"], st["gv_s"])
```

```python
import functools
import math

import numpy as np
import jax
import jax.numpy as jnp
from jax import lax
from jax.experimental import pallas as pl
from jax.experimental.pallas import tpu as pltpu

F32 = jnp.float32
BF16 = jnp.bfloat16

D_MODEL = 1024
EPS = 1e-6
NEG_INF = -1e30
MLA_HEADS = 8
Q_LORA = 384
KV_LORA = 256
QK_NOPE = 64
QK_ROPE = 32
V_HEAD = 64
ROPE_THETA = 10000.0
MLA_SCALE = 1.0 / math.sqrt(QK_NOPE + QK_ROPE)
LOG2E = math.log2(math.e)
SSM_HEADS = 8
SSM_HEAD_DIM = 64
D_INNER = SSM_HEADS * SSM_HEAD_DIM
SSM_GROUPS = 2
D_STATE = 64
CONV_K = 4
CONV_DIM = D_INNER + 2 * SSM_GROUPS * D_STATE
GM_GROUPS = 8
GM_GROUP_DIM = 64
GM_WIDTH = GM_GROUPS * GM_GROUP_DIM
MEM_HEADS = 4
MEM_HEAD_DIM = 64
MEM_WIDTH = MEM_HEADS * MEM_HEAD_DIM
MEM_SCALE = 1.0 / math.sqrt(MEM_HEAD_DIM)
D_FF = 2816
IN_SIZES = (Q_LORA, KV_LORA, QK_ROPE, D_INNER, CONV_DIM, SSM_HEADS, GM_WIDTH, GM_WIDTH)

LANES = 128
CHUNK = 128
HEAD_PAD = LANES
ROPE_LO = QK_NOPE
ROPE_HALF = QK_ROPE // 2
MISC_DT = 96
VMEM_LIMIT = 56 * 1024 * 1024

P_CQ, P_CKV, P_Z, P_XBC, P_U, P_V, P_MISC, P_END = 0, 384, 640, 1152, 1920, 2432, 2944, 3072

NT_DIMS = (((1,), (1,)), ((), ()))
TN_DIMS = (((0,), (0,)), ((), ()))


def _pick(n, cands):
    for c in cands:
        if n % c == 0:
            return c
    raise ValueError(f"no tile in {cands} divides {n}")


def _rms(x, g):
    return x * lax.rsqrt(jnp.mean(x * x, axis=-1, keepdims=True) + EPS) * g


def _silu(x):
    return x * (1.0 / (1.0 + jnp.exp(-x)))


def _dot(a, b):
    return jnp.dot(a, b, preferred_element_type=F32)


def _dot_nt(a, b):
    return lax.dot_general(a, b, NT_DIMS, preferred_element_type=F32)


def _dot_exact(a, b, dims=None):
    if dims is None:
        return jnp.dot(a, b, precision=lax.Precision.HIGHEST, preferred_element_type=F32)
    return lax.dot_general(a, b, dims, precision=lax.Precision.HIGHEST, preferred_element_type=F32)


def _lane_tile(x, n):
    return x if n == 1 else jnp.concatenate([x] * n, axis=1)


def _const_spec(shape):
    nd = len(shape)
    return pl.BlockSpec(shape, lambda *_: (0,) * nd)


def _resident_spec(shape):
    nd = len(shape)
    return pl.BlockSpec(shape, lambda *_: (0,) * nd, pipeline_mode=pl.Buffered(1))


def _params(sem):
    return pltpu.CompilerParams(dimension_semantics=sem, vmem_limit_bytes=VMEM_LIMIT)


FF_SPLIT = 2


def _ffn_body(*refs, has_extra, has_final, npb):
    it = iter(refs)
    x_ref = next(it)
    if has_extra:
        ep_ref, es_ref, we_ref = next(it), next(it), next(it)
    g_ref, win_ref, wout_ref = next(it), next(it), next(it)
    if has_final:
        fg_ref = next(it)
    o_ref = next(it)
    if has_final:
        y_ref = next(it)

    def compute(e_ref):
        x = x_ref[...]
        if has_extra:
            x = x + _dot(e_ref[...].astype(BF16), we_ref[...])
        h = _rms(x, g_ref[...]).astype(BF16)
        cw = D_FF // FF_SPLIT
        acc = jnp.zeros_like(x)
        for c in range(FF_SPLIT):
            gate = _dot(h, win_ref[:, c * cw:(c + 1) * cw])
            up = _dot(h, win_ref[:, D_FF + c * cw:D_FF + (c + 1) * cw])
            act = (_silu(gate) * up).astype(BF16)
            acc = acc + _dot(act, wout_ref[c * cw:(c + 1) * cw, :])
        out = x + 0.5 * acc
        o_ref[...] = out
        if has_final:
            y_ref[...] = _rms(out, fg_ref[...])

    if has_extra:
        @pl.when(pl.program_id(0) < npb)
        def _():
            compute(ep_ref)

        @pl.when(pl.program_id(0) >= npb)
        def _():
            compute(es_ref)
    else:
        compute(None)


def _ffn(x, norm_g, w_in, w_out, extra=None, w_extra=None, final_g=None, tp=0):
    t = x.shape[0]
    has_extra = extra is not None
    has_final = final_g is not None
    tm = _pick(math.gcd(tp, t - tp) if has_extra else t, (512, 256, 128, 64, 32, 16, 8))
    row = lambda i: (i, 0)
    args, specs = [x], [pl.BlockSpec((tm, D_MODEL), row)]
    if has_extra:
        prow, srow = _group_row_maps(tp // tm)
        args += [extra[0], extra[1], w_extra]
        specs += [pl.BlockSpec((tm, extra[0].shape[1]), prow), pl.BlockSpec((tm, extra[1].shape[1]), srow),
                  _resident_spec(w_extra.shape)]
    args += [norm_g.reshape(1, D_MODEL), w_in, w_out]
    specs += [_const_spec((1, D_MODEL)), _resident_spec(w_in.shape), _resident_spec(w_out.shape)]
    out_shape = [jax.ShapeDtypeStruct((t, D_MODEL), F32)]
    out_specs = [pl.BlockSpec((tm, D_MODEL), row)]
    if has_final:
        args.append(final_g.reshape(1, D_MODEL))
        specs.append(_const_spec((1, D_MODEL)))
        out_shape.append(jax.ShapeDtypeStruct((t, D_MODEL), F32))
        out_specs.append(pl.BlockSpec((tm, D_MODEL), row))
    res = pl.pallas_call(
        functools.partial(_ffn_body, has_extra=has_extra, has_final=has_final, npb=tp // tm),
        grid=(t // tm,), in_specs=specs, out_specs=out_specs, out_shape=out_shape,
        input_output_aliases={0: 0}, compiler_params=_params(("parallel",)), name="ffn",
    )(*args)
    return res if has_final else res[0]


def _rope(blk, c, s1, s2):
    return blk * c + pltpu.roll(blk, ROPE_HALF, 1) * s1 + pltpu.roll(blk, LANES - ROPE_HALF, 1) * s2


def _mixin_body(x_ref, g_ref, win_ref, qn_ref, wuq_ref, kvn_ref, wuk_ref, c_ref, s1_ref, s2_ref,
                q_ref, k_ref, ckv_ref, ckvb_ref, krp_ref, misc_ref, z_ref, xbc_ref, u_ref, v_ref):
    h = _rms(x_ref[...], g_ref[...]).astype(BF16)
    p = _dot(h, win_ref[...])
    z_ref[...] = p[:, P_Z:P_XBC]
    xbc_ref[...] = p[:, P_XBC:P_U]
    u_ref[...] = p[:, P_U:P_V]
    v_ref[...] = p[:, P_V:P_MISC]
    misc = p[:, P_MISC:P_END]
    misc_ref[...] = misc
    c, s1, s2 = c_ref[...], s1_ref[...], s2_ref[...]
    krp = _rope(misc, c, s1, s2)
    krp_ref[...] = krp
    ckvn = _rms(p[:, P_CKV:P_Z], kvn_ref[...])
    ckv_ref[...] = ckvn
    ckvb = ckvn.astype(BF16)
    ckvb_ref[...] = ckvb
    kn = _dot(ckvb, wuk_ref[...])
    qn = _rms(p[:, P_CQ:P_CKV], qn_ref[...]).astype(BF16)
    q = _dot(qn, wuq_ref[...]) * (MLA_SCALE * LOG2E)
    for hh in range(MLA_HEADS):
        sl = slice(hh * HEAD_PAD, (hh + 1) * HEAD_PAD)
        q_ref[hh] = _rope(q[:, sl], c, s1, s2).astype(BF16)
        k_ref[hh] = (kn[:, sl] + krp).astype(BF16)


def _mixin(x, lw, tabs):
    t = x.shape[0]
    tm = _pick(t, (512, 256, 128, 64, 32, 16))
    row = lambda i: (i, 0)
    hrow = lambda i: (0, i, 0)
    wide = MLA_HEADS * HEAD_PAD
    outs = [
        ((MLA_HEADS, t, HEAD_PAD), BF16, pl.BlockSpec((MLA_HEADS, tm, HEAD_PAD), hrow)),
        ((MLA_HEADS, t, HEAD_PAD), BF16, pl.BlockSpec((MLA_HEADS, tm, HEAD_PAD), hrow)),
        ((t, KV_LORA), F32, pl.BlockSpec((tm, KV_LORA), row)),
        ((t, KV_LORA), BF16, pl.BlockSpec((tm, KV_LORA), row)),
        ((t, LANES), F32, pl.BlockSpec((tm, LANES), row)),
        ((t, LANES), F32, pl.BlockSpec((tm, LANES), row)),
        ((t, D_INNER), F32, pl.BlockSpec((tm, D_INNER), row)),
        ((t, CONV_DIM), F32, pl.BlockSpec((tm, CONV_DIM), row)),
        ((t, GM_WIDTH), F32, pl.BlockSpec((tm, GM_WIDTH), row)),
        ((t, GM_WIDTH), F32, pl.BlockSpec((tm, GM_WIDTH), row)),
    ]
    return pl.pallas_call(
        _mixin_body, grid=(t // tm,),
        in_specs=[pl.BlockSpec((tm, D_MODEL), row), _const_spec((1, D_MODEL)),
                  _resident_spec((D_MODEL, P_END)), _const_spec((1, Q_LORA)),
                  _resident_spec((Q_LORA, wide)), _const_spec((1, KV_LORA)),
                  _resident_spec((KV_LORA, wide)),
                  pl.BlockSpec((tm, LANES), row), pl.BlockSpec((tm, LANES), row),
                  pl.BlockSpec((tm, LANES), row)],
        out_specs=[o[2] for o in outs],
        out_shape=[jax.ShapeDtypeStruct(o[0], o[1]) for o in outs],
        compiler_params=_params(("parallel",)), name="mixin",
    )(x, lw["mix_norm"], lw["w_in_p"], lw["q_norm"], lw["w_uq_p"], lw["kv_norm"], lw["w_uk_p"], *tabs)


def _pattn_body(qi_ref, kj_ref, q_ref, k_ref, v_ref, wuv_ref, o_ref, m_ref, l_ref, acc_ref, *, tq, kvb):
    n = pl.program_id(1)
    i = qi_ref[n]
    j = kj_ref[n]
    last = (i * tq + tq - 1) // kvb

    @pl.when(j == 0)
    def _():
        m_ref[...] = jnp.full(m_ref.shape, NEG_INF, F32)
        l_ref[...] = jnp.zeros(l_ref.shape, F32)
        acc_ref[...] = jnp.zeros(acc_ref.shape, F32)

    def step(masked):
        v = v_ref[...]
        if masked:
            qpos = i * tq + lax.broadcasted_iota(jnp.int32, (tq, kvb), 0)
            kpos = j * kvb + lax.broadcasted_iota(jnp.int32, (tq, kvb), 1)
            ok = kpos <= qpos
        for hh in range(MLA_HEADS):
            s = _dot_nt(q_ref[hh], k_ref[hh])
            if masked:
                s = jnp.where(ok, s, NEG_INF)
            m_prev = m_ref[hh]
            m_new = jnp.maximum(m_prev, jnp.max(s, axis=1, keepdims=True))
            alpha = jnp.exp2(m_prev - m_new)
            p = jnp.exp2(s - _lane_tile(m_new, kvb // LANES))
            psum = p[:, :LANES]
            for t in range(1, kvb // LANES):
                psum = psum + p[:, t * LANES:(t + 1) * LANES]
            l_ref[hh] = alpha * l_ref[hh] + psum
            acc_ref[hh] = acc_ref[hh] * _lane_tile(alpha, KV_LORA // LANES) + _dot(p.astype(BF16), v)
            m_ref[hh] = m_new

    @pl.when(j == last)
    def _():
        step(True)
        for hh in range(MLA_HEADS):
            o = (acc_ref[hh] / jnp.sum(l_ref[hh], axis=1, keepdims=True)).astype(BF16)
            o_ref[:, hh * V_HEAD:(hh + 1) * V_HEAD] = _dot(o, wuv_ref[hh]).astype(BF16)

    @pl.when(j != last)
    def _():
        step(False)


def _prompt_attention(q, k, vb, w_uv, bsz, seq):
    tq = _pick(seq, (512, 256, 128))
    kvb = _pick(seq, (512, 256, 128))
    nq, nkv = seq // tq, seq // kvb
    pairs = [(i, j) for i in range(nq) for j in range((i * tq + tq - 1) // kvb + 1)]
    qi = jnp.asarray(np.array([p[0] for p in pairs], np.int32))
    kj = jnp.asarray(np.array([p[1] for p in pairs], np.int32))
    grid_spec = pltpu.PrefetchScalarGridSpec(
        num_scalar_prefetch=2, grid=(bsz, len(pairs)),
        in_specs=[
            pl.BlockSpec((MLA_HEADS, tq, HEAD_PAD), lambda b, n, qi, kj: (0, b * nq + qi[n], 0)),
            pl.BlockSpec((MLA_HEADS, kvb, HEAD_PAD), lambda b, n, qi, kj: (0, b * nkv + kj[n], 0)),
            pl.BlockSpec((kvb, KV_LORA), lambda b, n, qi, kj: (b * nkv + kj[n], 0)),
            pl.BlockSpec((MLA_HEADS, KV_LORA, V_HEAD), lambda b, n, qi, kj: (0, 0, 0)),
        ],
        out_specs=pl.BlockSpec((tq, MLA_HEADS * V_HEAD), lambda b, n, qi, kj: (b * nq + qi[n], 0)),
        scratch_shapes=[pltpu.VMEM((MLA_HEADS, tq, LANES), F32), pltpu.VMEM((MLA_HEADS, tq, LANES), F32),
                        pltpu.VMEM((MLA_HEADS, tq, KV_LORA), F32)],
    )
    return pl.pallas_call(
        functools.partial(_pattn_body, tq=tq, kvb=kvb), grid_spec=grid_spec,
        out_shape=jax.ShapeDtypeStruct((bsz * seq, MLA_HEADS * V_HEAD), BF16),
        compiler_params=_params(("parallel", "arbitrary")), name="prompt_attn",
    )(qi, kj, q, k, vb, w_uv)


def _sattn_body(pt_ref, q_ref, ckvn_ref, krn_ref, wukt_ref, wuv_ref, ckv_hbm, ckr_hbm, o_ref,
                kbuf, rbuf, sem, qlat_ref, qrope_ref, m_ref, l_ref, acc_ref, *, layer, pc, n_seq, n_ch, page, nsub):
    b = pl.program_id(0)
    c = pl.program_id(1)
    step = b * n_ch + c
    slot = step % 2
    n_new = q_ref.shape[1] // MLA_HEADS

    def copies(bb, cc, sl):
        kv, kr = [], []
        for pg in range(pc):
            pid = pt_ref[bb, cc * pc + pg]
            kv.append(pltpu.make_async_copy(ckv_hbm.at[layer, pid], kbuf.at[sl, pl.ds(pg * page, page)], sem.at[0, sl]))
            kr.append(pltpu.make_async_copy(ckr_hbm.at[layer, pid], rbuf.at[sl, :, pl.ds(pg * page, page)], sem.at[1, sl]))
        return kv + kr

    @pl.when(step == 0)
    def _():
        for cp in copies(b, c, slot):
            cp.start()

    @pl.when(step + 1 < n_seq * n_ch)
    def _():
        wrap = c + 1 == n_ch
        nb = jnp.where(wrap, b + 1, b)
        nc = jnp.where(wrap, 0, c + 1)
        for cp in copies(nb, nc, 1 - slot):
            cp.start()

    @pl.when(c == 0)
    def _():
        m_ref[...] = jnp.full(m_ref.shape, NEG_INF, F32)
        l_ref[...] = jnp.zeros(l_ref.shape, F32)
        acc_ref[...] = jnp.zeros(acc_ref.shape, F32)
        q32 = q_ref[0].astype(F32)
        for hh in range(MLA_HEADS):
            qh = q32[hh * n_new:(hh + 1) * n_new, :].astype(BF16)
            qlat_ref[hh * n_new:(hh + 1) * n_new, :] = _dot(qh, wukt_ref[hh])
        qrope_ref[...] = q32[:, ROPE_LO:ROPE_LO + QK_ROPE]

    def part(s, vals):
        m = jnp.max(s, axis=1, keepdims=True)
        p = jnp.exp2(s - m)
        return m, jnp.sum(p, axis=1, keepdims=True), _dot(p.astype(BF16), vals)

    def merge(parts):
        m_prev = m_ref[...]
        m_new = m_prev
        for m, _, _ in parts:
            m_new = jnp.maximum(m_new, m)
        alpha = jnp.exp2(m_prev - m_new)
        l_new = alpha * l_ref[...]
        acc = acc_ref[...] * alpha
        for m, l, a in parts:
            w = jnp.exp2(m - m_new)
            l_new = l_new + w * l
            acc = acc + w * a
        m_ref[...] = m_new
        l_ref[...] = l_new
        acc_ref[...] = acc

    for cp in copies(b, c, slot):
        cp.wait()
    qlat = qlat_ref[...].astype(BF16)
    qrope = qrope_ref[...].astype(BF16)
    keys = pc * page // nsub
    parts = []
    for u in range(nsub):
        kc = kbuf[slot, u * keys:(u + 1) * keys, :].astype(BF16)
        rc = rbuf[slot, :, u * keys:(u + 1) * keys].astype(BF16)
        parts.append(part(_dot_nt(qlat, kc) + _dot(qrope, rc), kc))
    merge(parts)

    @pl.when(c == n_ch - 1)
    def _():
        pad = LANES - n_new
        kn = jnp.concatenate([ckvn_ref[...], jnp.zeros((pad, KV_LORA), F32)], axis=0).astype(BF16)
        rn = jnp.concatenate([krn_ref[...][:, ROPE_LO:ROPE_LO + QK_ROPE], jnp.zeros((pad, QK_ROPE), F32)],
                             axis=0).astype(BF16)
        s = _dot_nt(qlat, kn) + _dot_nt(qrope, rn)
        tpos = lax.broadcasted_iota(jnp.int32, s.shape, 0) % n_new
        kpos = lax.broadcasted_iota(jnp.int32, s.shape, 1)
        merge([part(jnp.where(kpos <= tpos, s, NEG_INF), kn)])
        o = acc_ref[...] / l_ref[...]
        pieces = [_dot(o[hh * n_new:(hh + 1) * n_new, :].astype(BF16), wuv_ref[hh]) for hh in range(MLA_HEADS)]
        o_ref[0] = jnp.concatenate(pieces, axis=1).astype(BF16)


SATTN_PAGES = 32
SATTN_SPLIT = 2


def _sample_attention(q_s, ckv_all, krp_all, w_ukt_p, w_uv, cache_kv, cache_kr_t, page_table, layer, tp, n_new):
    n_seq, n_pages = page_table.shape
    page = cache_kv.shape[2]
    pc = _pick(n_pages, (SATTN_PAGES, 16, 8, 4, 2, 1))
    nsub = SATTN_SPLIT if pc % SATTN_SPLIT == 0 else 1
    n_ch = n_pages // pc
    rows = MLA_HEADS * n_new
    off = tp // n_new
    grid_spec = pltpu.PrefetchScalarGridSpec(
        num_scalar_prefetch=1, grid=(n_seq, n_ch),
        in_specs=[
            pl.BlockSpec((1, rows, HEAD_PAD), lambda b, c, pt: (b, 0, 0)),
            pl.BlockSpec((n_new, KV_LORA), lambda b, c, pt: (off + b, 0)),
            pl.BlockSpec((n_new, LANES), lambda b, c, pt: (off + b, 0)),
            pl.BlockSpec((MLA_HEADS, HEAD_PAD, KV_LORA), lambda b, c, pt: (0, 0, 0)),
            pl.BlockSpec((MLA_HEADS, KV_LORA, V_HEAD), lambda b, c, pt: (0, 0, 0)),
            pl.BlockSpec(memory_space=pl.ANY),
            pl.BlockSpec(memory_space=pl.ANY),
        ],
        out_specs=pl.BlockSpec((1, n_new, MLA_HEADS * V_HEAD), lambda b, c, pt: (b, 0, 0)),
        scratch_shapes=[
            pltpu.VMEM((2, pc * page, KV_LORA), F32), pltpu.VMEM((2, QK_ROPE, pc * page), F32),
            pltpu.SemaphoreType.DMA((2, 2)),
            pltpu.VMEM((rows, KV_LORA), F32), pltpu.VMEM((rows, QK_ROPE), F32),
            pltpu.VMEM((rows, 1), F32), pltpu.VMEM((rows, 1), F32), pltpu.VMEM((rows, KV_LORA), F32),
        ],
    )
    out = pl.pallas_call(
        functools.partial(_sattn_body, layer=layer, pc=pc, n_seq=n_seq, n_ch=n_ch, page=page, nsub=nsub),
        grid_spec=grid_spec,
        out_shape=jax.ShapeDtypeStruct((n_seq, n_new, MLA_HEADS * V_HEAD), BF16),
        compiler_params=_params(("arbitrary", "arbitrary")), name="sample_attn",
    )(page_table, q_s, ckv_all, krp_all, w_ukt_p, w_uv, cache_kv, cache_kr_t)
    return out.reshape(n_seq * n_new, MLA_HEADS * V_HEAD)


def _ssd_consts(seg):
    r = np.arange(CHUNK)
    same = (r[:, None] // seg) == (r[None, :] // seg)
    caus = (same & (r[None, :] <= r[:, None])).astype(np.float32)
    plast = (r[None, :] == (r[:, None] // seg) * seg + seg - 1).astype(np.float32)
    et = np.zeros((D_INNER, LANES), np.float32)
    et[np.arange(D_INNER), MISC_DT + np.arange(D_INNER) // SSM_HEAD_DIM] = 1.0
    return jnp.asarray(caus), jnp.asarray(plast), jnp.asarray(et)


def _ssd_tile(act, z, misc, caus, plast, et, dtb, alog, dskip, ng, h_in_fn, seg):
    nseg = CHUNK // seg
    lane = lax.broadcasted_iota(jnp.int32, (1, LANES), 1)
    hmask = (lane >= MISC_DT) & (lane < MISC_DT + SSM_HEADS)
    lo = lax.broadcasted_iota(jnp.int32, (CHUNK, LANES), 1) < SSM_HEAD_DIM
    xs = act[:, :D_INNER]
    bm = act[:, D_INNER:D_INNER + LANES]
    cm = act[:, D_INNER + LANES:D_INNER + 2 * LANES]
    dtv = jnp.where(hmask, jax.nn.softplus(misc + dtb), 0.0)
    a_row = jnp.where(hmask, -jnp.exp(alog), 0.0)
    acs = _dot_exact(caus, dtv * a_row)
    acs_last = _dot_exact(plast, acs)
    e_acs = jnp.exp(acs)
    to_end = jnp.exp(acs_last - acs)
    e_last = jnp.exp(acs_last)
    acs_t = acs.T
    dec_all = _dot_exact(et, e_last, NT_DIMS)

    bm_b = bm.astype(BF16)
    cm_g = [jnp.where(lo, cm, 0.0).astype(BF16), jnp.where(lo, 0.0, cm).astype(BF16)]
    g_mat = [_dot_nt(cm_g[g], bm_b) for g in range(SSM_GROUPS)]
    cm_half = [cm[:, :D_STATE].astype(BF16), cm[:, D_STATE:].astype(BF16)]

    def col(arr, hh):
        return arr[:, MISC_DT + hh:MISC_DT + hh + 1]

    def pair(arr, pb):
        return jnp.where(lo, col(arr, 2 * pb), col(arr, 2 * pb + 1))

    hpg = SSM_HEADS // SSM_GROUPS
    rows_g = hpg * SSM_HEAD_DIM
    y_off = []
    for g in range(SSM_GROUPS):
        parts = []
        for j in range(nseg):
            hj = h_in_fn(j)[g * rows_g:(g + 1) * rows_g, :].astype(BF16)
            parts.append(_dot_nt(cm_half[g][j * seg:(j + 1) * seg, :], hj))
        y_off.append(parts[0] if nseg == 1 else jnp.concatenate(parts, axis=0))

    y_blocks, xte_blocks = [], []
    for pb in range(SSM_HEADS // 2):
        g = (2 * pb) // hpg
        sl = slice(pb * LANES, (pb + 1) * LANES)
        xs_p = xs[:, sl]
        xdt = xs_p * pair(dtv, pb)
        xte_blocks.append((xdt * pair(to_end, pb)).astype(BF16))
        xdt_b = xdt.astype(BF16)
        yd = []
        for hh in (2 * pb, 2 * pb + 1):
            segm = col(acs, hh) - acs_t[MISC_DT + hh:MISC_DT + hh + 1, :]
            decay = jnp.exp(jnp.where(caus > 0.0, segm, NEG_INF))
            yd.append(_dot((g_mat[g] * decay).astype(BF16), xdt_b))
        y_d = jnp.where(lo, yd[0], yd[1])
        off_sl = slice((pb % (hpg // 2)) * LANES, (pb % (hpg // 2) + 1) * LANES)
        y_o = y_off[g][:, off_sl] * pair(e_acs, pb)
        y_blocks.append(y_d + y_o + dskip[:, sl] * xs_p)
    y = jnp.concatenate(y_blocks, axis=1)
    yn = _rms(y * _silu(z), ng)

    xte = jnp.concatenate(xte_blocks, axis=1)
    rowi = lax.broadcasted_iota(jnp.int32, (D_INNER, D_STATE), 0) < rows_g
    new_states = []
    if nseg == 1:
        s_all = lax.dot_general(xte, bm_b, TN_DIMS, preferred_element_type=F32)
        s_sel = jnp.where(rowi, s_all[:, :D_STATE], s_all[:, D_STATE:])
        new_states.append(h_in_fn(0) * dec_all[:, 0:1] + s_sel)
    else:
        tr = lax.broadcasted_iota(jnp.int32, (CHUNK, LANES), 0) // seg
        bbd = jnp.concatenate([jnp.where(tr == j, bm, 0.0) for j in range(nseg)], axis=1).astype(BF16)
        s_all = lax.dot_general(xte, bbd, TN_DIMS, preferred_element_type=F32)
        for j in range(nseg):
            sj = s_all[:, j * LANES:(j + 1) * LANES]
            s_sel = jnp.where(rowi, sj[:, :D_STATE], sj[:, D_STATE:])
            new_states.append(h_in_fn(j) * dec_all[:, j * seg:j * seg + 1] + s_sel)
    return yn, new_states


def _ssd_prompt_body(z_ref, xbc_ref, misc_ref, caus_ref, plast_ref, et_ref, cw_ref, cb_ref, dtb_ref, alog_ref,
                     dskip_ref, ng_ref, y_ref, st_ref, xpad_ref, h_ref):
    c = pl.program_id(1)
    hist = CONV_K - 1

    @pl.when(c == 0)
    def _():
        xpad_ref[0:8, :] = jnp.zeros((8, CONV_DIM), F32)
        h_ref[...] = jnp.zeros(h_ref.shape, F32)

    x = xbc_ref[...]
    xpad_ref[8:8 + CHUNK, :] = x
    conv = cb_ref[...] + x * cw_ref[hist:hist + 1, :]
    for k in range(hist):
        conv = conv + xpad_ref[8 - hist + k:8 - hist + k + CHUNK, :] * cw_ref[k:k + 1, :]
    xpad_ref[0:8, :] = x[CHUNK - 8:, :]
    act = _silu(conv)
    yn, new = _ssd_tile(act, z_ref[...], misc_ref[...], caus_ref[...], plast_ref[...], et_ref[...],
                        dtb_ref[...], alog_ref[...], dskip_ref[...], ng_ref[...], lambda j: h_ref[...], CHUNK)
    y_ref[...] = yn.astype(BF16)
    h_ref[...] = new[0]

    @pl.when(c == pl.num_programs(1) - 1)
    def _():
        st_ref[0] = new[0]


def _ssd_sample_body(z_ref, xbc_ref, misc_ref, sp_ref, h0_ref, caus_ref, plast_ref, et_ref, cw_ref, cb_ref,
                     dtb_ref, alog_ref, dskip_ref, ng_ref, y_ref, st_ref, *, seg):
    hist = CONV_K - 1
    x = xbc_ref[...]
    sp = sp_ref[...]
    tpos = lax.broadcasted_iota(jnp.int32, (CHUNK, CONV_DIM), 0) % seg
    conv = cb_ref[...] + x * cw_ref[hist:hist + 1, :]
    for j in range(1, CONV_K):
        xin = pltpu.roll(x, j, 0)
        up = hist - j
        sin = sp if up == 0 else pltpu.roll(sp, CHUNK - up, 0)
        conv = conv + jnp.where(tpos >= j, xin, sin) * cw_ref[hist - j:hist - j + 1, :]
    act = _silu(conv)
    yn, new = _ssd_tile(act, z_ref[...], misc_ref[...], caus_ref[...], plast_ref[...], et_ref[...],
                        dtb_ref[...], alog_ref[...], dskip_ref[...], ng_ref[...], lambda j: h0_ref[j], seg)
    y_ref[...] = yn.astype(BF16)
    for j in range(len(new)):
        st_ref[j] = new[j]


def _ssd_param_args(lw):
    return [lw["conv_w"], lw["conv_b"], lw["dtb_row"], lw["alog_row"], lw["dskip_exp"], lw["ssm_norm"]]


def _ssd_param_specs():
    return [_const_spec((CONV_K, CONV_DIM)), _const_spec((1, CONV_DIM)), _const_spec((1, LANES)),
            _const_spec((1, LANES)), _const_spec((1, D_INNER)), _const_spec((1, D_INNER))]


def _ssd_prompt(z, xbc, misc, lw, bsz, seq):
    nc = seq // CHUNK
    caus, plast, et = _ssd_consts(CHUNK)
    row = lambda b, c: (b * nc + c, 0)
    y, st = pl.pallas_call(
        _ssd_prompt_body, grid=(bsz, nc),
        in_specs=[pl.BlockSpec((CHUNK, D_INNER), row), pl.BlockSpec((CHUNK, CONV_DIM), row),
                  pl.BlockSpec((CHUNK, LANES), row), _const_spec((CHUNK, CHUNK)), _const_spec((CHUNK, CHUNK)),
                  _const_spec((D_INNER, LANES))] + _ssd_param_specs(),
        out_specs=[pl.BlockSpec((CHUNK, D_INNER), row),
                   pl.BlockSpec((1, D_INNER, D_STATE), lambda b, c: (b, 0, 0))],
        out_shape=[jax.ShapeDtypeStruct((bsz * seq, D_INNER), BF16),
                   jax.ShapeDtypeStruct((bsz, D_INNER, D_STATE), F32)],
        scratch_shapes=[pltpu.VMEM((8 + CHUNK, CONV_DIM), F32), pltpu.VMEM((D_INNER, D_STATE), F32)],
        compiler_params=_params(("parallel", "arbitrary")), name="ssd_prompt",
    )(z, xbc, misc, caus, plast, et, *_ssd_param_args(lw))
    return y, st


def _ssd_sample(z, xbc, misc, sp, h0, lw, tp, n_seq, seg):
    nseg = CHUNK // seg
    nt = n_seq // nseg
    off = tp // CHUNK
    caus, plast, et = _ssd_consts(seg)
    row = lambda i: (off + i, 0)
    y, st = pl.pallas_call(
        functools.partial(_ssd_sample_body, seg=seg), grid=(nt,),
        in_specs=[pl.BlockSpec((CHUNK, D_INNER), row), pl.BlockSpec((CHUNK, CONV_DIM), row),
                  pl.BlockSpec((CHUNK, LANES), row), pl.BlockSpec((CHUNK, CONV_DIM), lambda i: (i, 0)),
                  pl.BlockSpec((nseg, D_INNER, D_STATE), lambda i: (i, 0, 0)),
                  _const_spec((CHUNK, CHUNK)), _const_spec((CHUNK, CHUNK)),
                  _const_spec((D_INNER, LANES))] + _ssd_param_specs(),
        out_specs=[pl.BlockSpec((CHUNK, D_INNER), lambda i: (i, 0)),
                   pl.BlockSpec((nseg, D_INNER, D_STATE), lambda i: (i, 0, 0))],
        out_shape=[jax.ShapeDtypeStruct((n_seq * seg, D_INNER), BF16),
                   jax.ShapeDtypeStruct((n_seq, D_INNER, D_STATE), F32)],
        compiler_params=_params(("parallel",)), name="ssd_sample",
    )(z, xbc, misc, sp, h0, caus, plast, et, *_ssd_param_args(lw))
    return y, st


def _gmlp_body(u_ref, v_ref, w_ref, mask_ref, b_ref, ng_ref, o_ref, *vn_out, reps):
    lo = lax.broadcasted_iota(jnp.int32, (CHUNK, LANES), 1) < GM_GROUP_DIM
    mask = mask_ref[...] > 0.0
    wm = [jnp.where(mask, w_ref[g], 0.0).astype(BF16) for g in range(GM_GROUPS)]
    bias = b_ref[...]
    for r in range(reps):
        rs = slice(r * CHUNK, (r + 1) * CHUNK)
        ug = jax.nn.gelu(u_ref[rs, :])
        vn = _rms(jax.nn.gelu(v_ref[rs, :]), ng_ref[...])
        if vn_out:
            vn_out[0][rs, :] = vn
        vb = vn.astype(BF16)
        for pb in range(GM_GROUPS // 2):
            sl = slice(pb * LANES, (pb + 1) * LANES)
            s = jnp.where(lo, _dot(wm[2 * pb], vb[:, sl]), _dot(wm[2 * pb + 1], vb[:, sl])) + bias[:, sl]
            o_ref[rs, sl] = (ug[:, sl] * s).astype(BF16)


def _gmlp(u, v, w_sp, mask, bias_exp, norm_g, row_off, n_rows, want_vn):
    reps = _pick(n_rows // CHUNK, (4, 2, 1))
    tm = reps * CHUNK
    off = row_off // tm
    row = lambda i: (off + i, 0)
    out_shape = [jax.ShapeDtypeStruct((n_rows, GM_WIDTH), BF16)]
    out_specs = [pl.BlockSpec((tm, GM_WIDTH), lambda i: (i, 0))]
    if want_vn:
        out_shape.append(jax.ShapeDtypeStruct((n_rows, GM_WIDTH), F32))
        out_specs.append(pl.BlockSpec((tm, GM_WIDTH), lambda i: (i, 0)))
    return pl.pallas_call(
        functools.partial(_gmlp_body, reps=reps), grid=(n_rows // tm,),
        in_specs=[pl.BlockSpec((tm, GM_WIDTH), row), pl.BlockSpec((tm, GM_WIDTH), row),
                  _const_spec((GM_GROUPS, CHUNK, CHUNK)), _const_spec((CHUNK, CHUNK)),
                  _const_spec((CHUNK, GM_WIDTH)), _const_spec((1, GM_WIDTH))],
        out_specs=out_specs, out_shape=out_shape,
        compiler_params=_params(("parallel",)), name="gmlp",
    )(u, v, w_sp, mask, bias_exp, norm_g)


def _merge_body(x_ref, ohp_ref, ysp_ref, gmp_ref, ohs_ref, yss_ref, gms_ref, ng_ref, wg_ref, bg_ref, womla_ref,
                wossm_ref, wogm_ref, wout_ref, mqn_ref, wmq_ref, xo_ref, qm_ref, *, npb):
    def compute(oh_ref, ys_ref, gm_ref):
        x = x_ref[...]
        h = _rms(x, ng_ref[...]).astype(BF16)
        gates = _dot(h, wg_ref[...]) + bg_ref[...]
        gates = 1.0 / (1.0 + jnp.exp(-gates))
        mix = (gates[:, :D_MODEL] * _dot(oh_ref[...], womla_ref[...])
               + gates[:, D_MODEL:2 * D_MODEL] * _dot(ys_ref[...], wossm_ref[...])
               + gates[:, 2 * D_MODEL:] * _dot(gm_ref[...], wogm_ref[...]))
        x2 = x + _dot(mix.astype(BF16), wout_ref[...])
        xo_ref[...] = x2
        hq = _rms(x2, mqn_ref[...]).astype(BF16)
        qm_ref[...] = _dot(hq, wmq_ref[...])

    @pl.when(pl.program_id(0) < npb)
    def _():
        compute(ohp_ref, ysp_ref, gmp_ref)

    @pl.when(pl.program_id(0) >= npb)
    def _():
        compute(ohs_ref, yss_ref, gms_ref)


def _group_row_maps(npb):
    return (lambda i: (jnp.minimum(i, npb - 1), 0)), (lambda i: (jnp.maximum(i - npb, 0), 0))


def _merge(x, branches_p, branches_s, lw, tp):
    t = x.shape[0]
    tm = _pick(math.gcd(tp, t - tp), (512, 256, 128, 64, 32, 16))
    row = lambda i: (i, 0)
    prow, srow = _group_row_maps(tp // tm)
    half = MLA_HEADS * V_HEAD
    widths = (half, D_INNER, GM_WIDTH)
    return pl.pallas_call(
        functools.partial(_merge_body, npb=tp // tm), grid=(t // tm,),
        in_specs=[pl.BlockSpec((tm, D_MODEL), row)]
        + [pl.BlockSpec((tm, w), prow) for w in widths] + [pl.BlockSpec((tm, w), srow) for w in widths]
        + [_const_spec((1, D_MODEL)), _resident_spec((D_MODEL, 3 * D_MODEL)), _const_spec((1, 3 * D_MODEL)),
           _resident_spec((half, D_MODEL)), _resident_spec((D_INNER, D_MODEL)),
           _resident_spec((GM_WIDTH, D_MODEL)), _resident_spec((D_MODEL, D_MODEL)),
           _const_spec((1, D_MODEL)), _resident_spec((D_MODEL, MEM_WIDTH))],
        out_specs=[pl.BlockSpec((tm, D_MODEL), row), pl.BlockSpec((tm, MEM_WIDTH), row)],
        out_shape=[jax.ShapeDtypeStruct((t, D_MODEL), F32), jax.ShapeDtypeStruct((t, MEM_WIDTH), F32)],
        input_output_aliases={0: 0}, compiler_params=_params(("parallel",)), name="merge",
    )(x, *branches_p, *branches_s, lw["mix_norm"], lw["w_gate"], lw["b_gate"], lw["w_o_mla"], lw["w_o_ssm"],
      lw["w_o_gm"], lw["w_out"], lw["mem_q_norm"], lw["w_mem_q"])


def _memattn_body(q_ref, k_ref, v_ref, o_ref, *, sb):
    m = k_ref.shape[1]
    q = q_ref[...]
    rows = q.shape[0] // sb
    kb = k_ref[...].reshape(sb * m, MEM_WIDTH).astype(BF16)
    vb = v_ref[...].reshape(sb * m, MEM_WIDTH).astype(BF16)
    lane = lax.broadcasted_iota(jnp.int32, q.shape, 1) // MEM_HEAD_DIM
    if sb > 1:
        own = (lax.broadcasted_iota(jnp.int32, (q.shape[0], sb * m), 0) // rows
               == lax.broadcasted_iota(jnp.int32, (q.shape[0], sb * m), 1) // m)
    out = jnp.zeros(q.shape, F32)
    for hh in range(MEM_HEADS):
        sel = lane == hh
        s = _dot_nt(jnp.where(sel, q, 0.0).astype(BF16), kb) * MEM_SCALE
        if sb > 1:
            s = jnp.where(own, s, NEG_INF)
        p = jnp.exp(s - jnp.max(s, axis=1, keepdims=True))
        p = p / jnp.sum(p, axis=1, keepdims=True)
        out = out + jnp.where(sel, _dot(p.astype(BF16), vb), 0.0)
    o_ref[...] = out


MEMATTN_BATCH = 8


def _mem_attention(qm, mem_k, mem_v, row_off, nb, rows_per_b):
    tm = _pick(rows_per_b, (512, 256, 128, 64, 32, 16, 8))
    nt = rows_per_b // tm
    sb = MEMATTN_BATCH if (nt == 1 and nb % MEMATTN_BATCH == 0 and tm * MEMATTN_BATCH <= 512) else 1
    off = row_off // (tm * sb)
    m = mem_k.shape[1]
    return pl.pallas_call(
        functools.partial(_memattn_body, sb=sb), grid=(nb // sb, nt),
        in_specs=[pl.BlockSpec((tm * sb, MEM_WIDTH), lambda b, i: (off + b * nt + i, 0)),
                  pl.BlockSpec((sb, m, MEM_WIDTH), lambda b, i: (b, 0, 0)),
                  pl.BlockSpec((sb, m, MEM_WIDTH), lambda b, i: (b, 0, 0))],
        out_specs=pl.BlockSpec((tm * sb, MEM_WIDTH), lambda b, i: (b * nt + i, 0)),
        out_shape=jax.ShapeDtypeStruct((nb * rows_per_b, MEM_WIDTH), F32),
        compiler_params=_params(("parallel", "parallel")), name="mem_attn",
    )(qm, mem_k, mem_v)


def _memkv_body(m_ref, g_ref, w_ref, o_ref):
    o_ref[...] = _dot(_rms(m_ref[...], g_ref[...]).astype(BF16), w_ref[...])


def _memory_kv(mem2d, norm_g, w_kv):
    t = mem2d.shape[0]
    tm = _pick(t, (256, 128, 64, 32, 16, 8))
    return pl.pallas_call(
        _memkv_body, grid=(t // tm,),
        in_specs=[pl.BlockSpec((tm, D_MODEL), lambda i: (i, 0)), _const_spec((1, D_MODEL)),
                  _const_spec((D_MODEL, 2 * MEM_WIDTH))],
        out_specs=pl.BlockSpec((tm, 2 * MEM_WIDTH), lambda i: (i, 0)),
        out_shape=jax.ShapeDtypeStruct((t, 2 * MEM_WIDTH), F32),
        compiler_params=_params(("parallel",)), name="mem_kv",
    )(mem2d, norm_g, w_kv)


def _layer_weights(l, p, seg):
    w_in = p["w_in"][l]
    offs = np.cumsum((0,) + IN_SIZES)
    cq, ckv, kr, z, xbc, dt, u, v = [w_in[:, offs[i]:offs[i + 1]] for i in range(len(IN_SIZES))]
    zc = lambda n: jnp.zeros((D_MODEL, n), F32)
    w_in_p = jnp.concatenate(
        [cq, ckv, z, xbc, u, v, zc(ROPE_LO), kr, dt, zc(LANES - ROPE_LO - QK_ROPE - SSM_HEADS)], axis=1)
    hd = QK_NOPE + QK_ROPE
    w_uq_p = jnp.pad(p["w_uq"][l].reshape(Q_LORA, MLA_HEADS, hd), ((0, 0), (0, 0), (0, HEAD_PAD - hd)))
    w_uk = p["w_uk"][l]
    w_uk_p = jnp.pad(jnp.transpose(w_uk, (1, 0, 2)), ((0, 0), (0, 0), (0, HEAD_PAD - QK_NOPE)))
    w_ukt_p = jnp.pad(jnp.transpose(w_uk, (0, 2, 1)), ((0, 0), (0, HEAD_PAD - QK_NOPE), (0, 0)))
    lane_pad = lambda a: jnp.pad(a.reshape(1, SSM_HEADS), ((0, 0), (MISC_DT, LANES - MISC_DT - SSM_HEADS)))
    w_sp = p["w_spatial"][l]
    b_sp = p["b_spatial"][l]
    bias_p = jnp.repeat(b_sp.T, GM_GROUP_DIM, axis=1)
    reps = CHUNK // seg
    w_sp_s = jnp.tile(w_sp[:, :seg, :seg], (1, reps, reps))
    bias_s = jnp.tile(jnp.repeat(b_sp[:, :seg].T, GM_GROUP_DIM, axis=1), (reps, 1))
    row = lambda a: a.reshape(1, -1)
    return {
        "ffn_norm": p["ffn_norm"][l], "w_ffn_in": p["w_ffn_in"][l].astype(BF16),
        "w_ffn_out": p["w_ffn_out"][l].astype(BF16),
        "mix_norm": row(p["mix_norm"][l]), "w_in_p": w_in_p.astype(BF16),
        "q_norm": row(p["q_norm"][l]), "w_uq_p": w_uq_p.reshape(Q_LORA, MLA_HEADS * HEAD_PAD).astype(BF16),
        "kv_norm": row(p["kv_norm"][l]), "w_uk_p": w_uk_p.reshape(KV_LORA, MLA_HEADS * HEAD_PAD).astype(BF16),
        "w_ukt_p": w_ukt_p.astype(BF16), "w_uv": p["w_uv"][l].astype(BF16),
        "w_o_mla": p["w_o_mla"][l].astype(BF16),
        "conv_w": p["conv_w"][l], "conv_b": row(p["conv_b"][l]),
        "dtb_row": lane_pad(p["dt_bias"][l]), "alog_row": lane_pad(p["a_log"][l]),
        "dskip_exp": row(jnp.repeat(p["d_skip"][l], SSM_HEAD_DIM)), "ssm_norm": row(p["ssm_norm"][l]),
        "w_o_ssm": p["w_o_ssm"][l].astype(BF16),
        "gm_norm": row(p["gm_norm"][l]), "w_sp_p": w_sp, "bias_p": bias_p, "w_sp_s": w_sp_s, "bias_s": bias_s,
        "w_o_gm": p["w_o_gm"][l].astype(BF16),
        "w_gate": p["w_gate"][l].astype(BF16), "b_gate": row(p["b_gate"][l]), "w_out": p["w_out"][l].astype(BF16),
        "mem_q_norm": row(p["mem_q_norm"][l]), "mem_kv_norm": row(p["mem_kv_norm"][l]),
        "w_mem_q": p["w_mem_q"][l].astype(BF16), "w_mem_kv": p["w_mem_kv"][l].astype(BF16),
        "w_mem_o": p["w_mem_o"][l].astype(BF16),
    }


def _rope_tables(pos):
    inv = ROPE_THETA ** (-jnp.arange(ROPE_HALF, dtype=F32) / ROPE_HALF)
    ang = pos.astype(F32)[:, None] * inv[None, :]
    cos, sin = jnp.cos(ang), jnp.sin(ang)
    n = pos.shape[0]
    zero = lambda w: jnp.zeros((n, w), F32)
    tail = LANES - ROPE_LO - QK_ROPE
    c = jnp.concatenate([jnp.ones((n, ROPE_LO), F32), cos, cos, zero(tail)], axis=1)
    s1 = jnp.concatenate([zero(ROPE_LO + ROPE_HALF), sin, zero(tail)], axis=1)
    s2 = jnp.concatenate([zero(ROPE_LO), -sin, zero(ROPE_HALF + tail)], axis=1)
    return c, s1, s2


def _spatial_masks(seg):
    r = np.arange(CHUNK)
    same = (r[:, None] // seg) == (r[None, :] // seg)
    return jnp.asarray((same & (r[None, :] <= r[:, None])).astype(np.float32))


def kernel(x_prompt, x_sample, mem_prompt, cache_kv_latent, cache_k_rope, state_ssm, state_conv, cache_mem_k, cache_mem_v, page_table, ffn_norm, w_ffn_in, w_ffn_out, mix_norm, w_in, q_norm, w_uq, kv_norm, w_uk, w_uv, w_o_mla, conv_w, conv_b, dt_bias, a_log, d_skip, ssm_norm, w_o_ssm, gm_norm, w_spatial, b_spatial, w_o_gm, w_gate, b_gate, w_out, mem_q_norm, mem_kv_norm, w_mem_q, w_mem_kv, w_mem_o, final_norm):
    params = dict(ffn_norm=ffn_norm, w_ffn_in=w_ffn_in, w_ffn_out=w_ffn_out, mix_norm=mix_norm, w_in=w_in,
                  q_norm=q_norm, w_uq=w_uq, kv_norm=kv_norm, w_uk=w_uk, w_uv=w_uv, w_o_mla=w_o_mla,
                  conv_w=conv_w, conv_b=conv_b, dt_bias=dt_bias, a_log=a_log, d_skip=d_skip, ssm_norm=ssm_norm,
                  w_o_ssm=w_o_ssm, gm_norm=gm_norm, w_spatial=w_spatial, b_spatial=b_spatial, w_o_gm=w_o_gm,
                  w_gate=w_gate, b_gate=b_gate, w_out=w_out, mem_q_norm=mem_q_norm, mem_kv_norm=mem_kv_norm,
                  w_mem_q=w_mem_q, w_mem_kv=w_mem_kv, w_mem_o=w_mem_o)
    bsz, seq, _ = x_prompt.shape
    n_seq, n_new, _ = x_sample.shape
    depth = w_in.shape[0]
    past = page_table.shape[1] * cache_kv_latent.shape[2]
    tp, ts = bsz * seq, n_seq * n_new
    mem_tokens = mem_prompt.shape[1]
    assert seq % CHUNK == 0 and CHUNK % n_new == 0 and ts % CHUNK == 0 and n_new % 8 == 0

    pos = jnp.concatenate([jnp.tile(jnp.arange(seq, dtype=jnp.int32), bsz),
                           jnp.tile(past + jnp.arange(n_new, dtype=jnp.int32), n_seq)])
    tabs = _rope_tables(pos)
    mask_p, mask_s = _spatial_masks(CHUNK), _spatial_masks(n_new)
    hist = CONV_K - 1

    x = jnp.concatenate([x_prompt.reshape(tp, D_MODEL), x_sample.reshape(ts, D_MODEL)], axis=0)
    mem2d = mem_prompt.reshape(bsz * mem_tokens, D_MODEL)
    cache_kr_t = jnp.swapaxes(cache_k_rope, 2, 3)
    outs = {k: [] for k in ("kvl_p", "kr_p", "ssm_p", "conv_p", "mk_p", "mv_p", "kvl_s", "kr_s", "ssm_s", "conv_s", "gv_s")}
    lws = [_layer_weights(l, params, n_new) for l in range(depth)]
    x = _ffn(x, lws[0]["ffn_norm"][0], lws[0]["w_ffn_in"][0], lws[0]["w_ffn_out"][0])
    y = None
    for l in range(depth):
        lw = lws[l]
        q, k, ckv, ckvb, krp, misc, z, xbc, u, v = _mixin(x, lw, tabs)

        oh_p = _prompt_attention(q, k, ckvb, lw["w_uv"], bsz, seq)
        q_s = jnp.transpose(q[:, tp:, :].reshape(MLA_HEADS, n_seq, n_new, HEAD_PAD), (1, 0, 2, 3))
        oh_s = _sample_attention(q_s.reshape(n_seq, MLA_HEADS * n_new, HEAD_PAD), ckv, krp, lw["w_ukt_p"],
                                 lw["w_uv"], cache_kv_latent, cache_kr_t, page_table, l, tp, n_new)

        ys_p, st_p = _ssd_prompt(z, xbc, misc, lw, bsz, seq)
        sp = jnp.pad(state_conv[l], ((0, 0), (0, n_new - hist), (0, 0))).reshape(ts, CONV_DIM)
        h0 = state_ssm[l].reshape(n_seq, D_INNER, D_STATE)
        ys_s, st_s = _ssd_sample(z, xbc, misc, sp, h0, lw, tp, n_seq, n_new)

        (gm_p,) = _gmlp(u, v, lw["w_sp_p"], mask_p, lw["bias_p"], lw["gm_norm"], 0, tp, False)
        gm_s, vn_s = _gmlp(u, v, lw["w_sp_s"], mask_s, lw["bias_s"], lw["gm_norm"], tp, ts, True)

        x, qm = _merge(x, (oh_p, ys_p, gm_p), (oh_s, ys_s, gm_s), lw, tp)

        kv = _memory_kv(mem2d, lw["mem_kv_norm"], lw["w_mem_kv"])
        mk_p = kv[:, :MEM_WIDTH].reshape(bsz, mem_tokens, MEM_WIDTH)
        mv_p = kv[:, MEM_WIDTH:].reshape(bsz, mem_tokens, MEM_WIDTH)
        om_p = _mem_attention(qm, mk_p, mv_p, 0, bsz, seq)
        om_s = _mem_attention(qm, cache_mem_k[l].reshape(n_seq, -1, MEM_WIDTH),
                              cache_mem_v[l].reshape(n_seq, -1, MEM_WIDTH), tp, n_seq, n_new)
        om = (om_p, om_s)

        if l == depth - 1:
            x, y = _ffn(x, lw["ffn_norm"][1], lw["w_ffn_in"][1], lw["w_ffn_out"][1], extra=om,
                        w_extra=lw["w_mem_o"], final_g=final_norm, tp=tp)
        else:
            x = _ffn(x, lw["ffn_norm"][1], lw["w_ffn_in"][1], lw["w_ffn_out"][1], extra=om, w_extra=lw["w_mem_o"],
                     tp=tp)
            nw = lws[l + 1]
            x = _ffn(x, nw["ffn_norm"][0], nw["w_ffn_in"][0], nw["w_ffn_out"][0])

        outs["kvl_p"].append(ckv[:tp].reshape(bsz, seq, KV_LORA))
        outs["kr_p"].append(krp[:tp, ROPE_LO:ROPE_LO + QK_ROPE].reshape(bsz, seq, QK_ROPE))
        outs["ssm_p"].append(st_p.reshape(bsz, SSM_HEADS, SSM_HEAD_DIM, D_STATE))
        outs["conv_p"].append(xbc[:tp].reshape(bsz, seq, CONV_DIM)[:, seq - hist:])
        outs["mk_p"].append(mk_p.reshape(bsz, mem_tokens, MEM_HEADS, MEM_HEAD_DIM))
        outs["mv_p"].append(mv_p.reshape(bsz, mem_tokens, MEM_HEADS, MEM_HEAD_DIM))
        outs["kvl_s"].append(ckv[tp:].reshape(n_seq, n_new, KV_LORA))
        outs["kr_s"].append(krp[tp:, ROPE_LO:ROPE_LO + QK_ROPE].reshape(n_seq, n_new, QK_ROPE))
        outs["ssm_s"].append(st_s.reshape(n_seq, SSM_HEADS, SSM_HEAD_DIM, D_STATE))
        outs["conv_s"].append(xbc[tp:].reshape(n_seq, n_new, CONV_DIM)[:, n_new - hist:])
        outs["gv_s"].append(vn_s.reshape(n_seq, n_new, GM_WIDTH))

    st = {k_: jnp.stack(v_) for k_, v_ in outs.items()}
    return (y[:tp].reshape(bsz, seq, D_MODEL), y[tp:].reshape(n_seq, n_new, D_MODEL),
            st["kvl_p"], st["kr_p"], st["ssm_p"], st["conv_p"], st["mk_p"], st["mv_p"],
            st["kvl_s"], st["kr_s"], st["ssm_s"], st["conv_s"], st["gv_s"])
```

```python
import functools
import math

import numpy as np
import jax
import jax.numpy as jnp
from jax import lax
from jax.experimental import pallas as pl
from jax.experimental.pallas import tpu as pltpu

F32 = jnp.float32
BF16 = jnp.bfloat16

D_MODEL = 1024
EPS = 1e-6
NEG_INF = -1e30
MLA_HEADS = 8
Q_LORA = 384
KV_LORA = 256
QK_NOPE = 64
QK_ROPE = 32
V_HEAD = 64
ROPE_THETA = 10000.0
MLA_SCALE = 1.0 / math.sqrt(QK_NOPE + QK_ROPE)
LOG2E = math.log2(math.e)
SSM_HEADS = 8
SSM_HEAD_DIM = 64
D_INNER = SSM_HEADS * SSM_HEAD_DIM
SSM_GROUPS = 2
D_STATE = 64
CONV_K = 4
CONV_DIM = D_INNER + 2 * SSM_GROUPS * D_STATE
GM_GROUPS = 8
GM_GROUP_DIM = 64
GM_WIDTH = GM_GROUPS * GM_GROUP_DIM
MEM_HEADS = 4
MEM_HEAD_DIM = 64
MEM_WIDTH = MEM_HEADS * MEM_HEAD_DIM
MEM_SCALE = 1.0 / math.sqrt(MEM_HEAD_DIM)
D_FF = 2816
IN_SIZES = (Q_LORA, KV_LORA, QK_ROPE, D_INNER, CONV_DIM, SSM_HEADS, GM_WIDTH, GM_WIDTH)

LANES = 128
CHUNK = 128
HEAD_PAD = LANES
ROPE_LO = QK_NOPE
ROPE_HALF = QK_ROPE // 2
MISC_DT = 96
VMEM_LIMIT = 56 * 1024 * 1024

P_CQ, P_CKV, P_Z, P_XBC, P_U, P_V, P_MISC, P_END = 0, 384, 640, 1152, 1920, 2432, 2944, 3072

NT_DIMS = (((1,), (1,)), ((), ()))
TN_DIMS = (((0,), (0,)), ((), ()))


def _pick(n, cands):
    for c in cands:
        if n % c == 0:
            return c
    raise ValueError(f"no tile in {cands} divides {n}")


def _rms(x, g):
    return x * lax.rsqrt(jnp.mean(x * x, axis=-1, keepdims=True) + EPS) * g


def _silu(x):
    return x * (1.0 / (1.0 + jnp.exp(-x)))


def _dot(a, b):
    return jnp.dot(a, b, preferred_element_type=F32)


def _dot_nt(a, b):
    return lax.dot_general(a, b, NT_DIMS, preferred_element_type=F32)


def _dot_exact(a, b, dims=None):
    if dims is None:
        return jnp.dot(a, b, precision=lax.Precision.HIGHEST, preferred_element_type=F32)
    return lax.dot_general(a, b, dims, precision=lax.Precision.HIGHEST, preferred_element_type=F32)


def _lane_tile(x, n):
    return x if n == 1 else jnp.concatenate([x] * n, axis=1)


def _const_spec(shape):
    nd = len(shape)
    return pl.BlockSpec(shape, lambda *_: (0,) * nd)


def _resident_spec(shape):
    nd = len(shape)
    return pl.BlockSpec(shape, lambda *_: (0,) * nd, pipeline_mode=pl.Buffered(1))


def _params(sem):
    return pltpu.CompilerParams(dimension_semantics=sem, vmem_limit_bytes=VMEM_LIMIT)


FF_SPLIT = 2


def _ffn_body(*refs, x_grouped, has_extra, has_final, npb):
    it = iter(refs)
    x_refs = (next(it), next(it)) if x_grouped else (next(it),) * 2
    e_refs = (None, None)
    if has_extra:
        e_refs = (next(it), next(it))
        we_ref = next(it)
    g_ref, win_ref, wout_ref = next(it), next(it), next(it)
    if has_final:
        fg_ref = next(it)
    o_refs = (next(it), next(it)) if has_final else (next(it),) * 2

    def compute(x_ref, e_ref, o_ref):
        x = x_ref[...]
        if has_extra:
            x = x + _dot(e_ref[...].astype(BF16), we_ref[...])
        h = _rms(x, g_ref[...]).astype(BF16)
        cw = D_FF // FF_SPLIT
        acc = jnp.zeros_like(x)
        for c in range(FF_SPLIT):
            gate = _dot(h, win_ref[:, c * cw:(c + 1) * cw])
            up = _dot(h, win_ref[:, D_FF + c * cw:D_FF + (c + 1) * cw])
            act = (_silu(gate) * up).astype(BF16)
            acc = acc + _dot(act, wout_ref[c * cw:(c + 1) * cw, :])
        out = x + 0.5 * acc
        o_ref[...] = _rms(out, fg_ref[...]) if has_final else out

    if x_grouped or has_extra or has_final:
        @pl.when(pl.program_id(0) < npb)
        def _():
            compute(x_refs[0], e_refs[0], o_refs[0])

        @pl.when(pl.program_id(0) >= npb)
        def _():
            compute(x_refs[1], e_refs[1], o_refs[1])
    else:
        compute(x_refs[0], None, o_refs[0])


def _ffn(x, norm_g, w_in_all, w_out_all, layer, which, tp, extra=None, w_extra=None, final_g=None):
    x_grouped = isinstance(x, tuple)
    t = x[0].shape[0] + x[1].shape[0] if x_grouped else x.shape[0]
    has_extra = extra is not None
    has_final = final_g is not None
    tm = _pick(math.gcd(tp, t - tp), (512, 256, 128, 64, 32, 16, 8))
    npb = tp // tm
    row = lambda i: (i, 0)
    prow, srow = _group_row_maps(npb)
    if x_grouped:
        args = [x[0], x[1]]
        specs = [pl.BlockSpec((tm, D_MODEL), prow), pl.BlockSpec((tm, D_MODEL), srow)]
    else:
        args, specs = [x], [pl.BlockSpec((tm, D_MODEL), row)]
    if has_extra:
        args += [extra[0], extra[1], w_extra]
        specs += [pl.BlockSpec((tm, extra[0].shape[1]), prow), pl.BlockSpec((tm, extra[1].shape[1]), srow),
                  _resident_spec(w_extra.shape)]
    pick = lambda i: (layer, which, 0, 0)
    args += [norm_g.reshape(1, D_MODEL), w_in_all, w_out_all]
    specs += [_const_spec((1, D_MODEL)),
              pl.BlockSpec((None, None) + w_in_all.shape[2:], pick, pipeline_mode=pl.Buffered(1)),
              pl.BlockSpec((None, None) + w_out_all.shape[2:], pick, pipeline_mode=pl.Buffered(1))]
    if has_final:
        args.append(final_g.reshape(1, D_MODEL))
        specs.append(_const_spec((1, D_MODEL)))
        out_shape = [jax.ShapeDtypeStruct((tp, D_MODEL), F32), jax.ShapeDtypeStruct((t - tp, D_MODEL), F32)]
        out_specs = [pl.BlockSpec((tm, D_MODEL), prow), pl.BlockSpec((tm, D_MODEL), srow)]
    else:
        out_shape = [jax.ShapeDtypeStruct((t, D_MODEL), F32)]
        out_specs = [pl.BlockSpec((tm, D_MODEL), row)]
    res = pl.pallas_call(
        functools.partial(_ffn_body, x_grouped=x_grouped, has_extra=has_extra, has_final=has_final, npb=npb),
        grid=(t // tm,), in_specs=specs, out_specs=out_specs, out_shape=out_shape,
        input_output_aliases={} if (x_grouped or has_final) else {0: 0},
        compiler_params=_params(("arbitrary",) if has_final else ("parallel",)), name="ffn",
    )(*args)
    return res if has_final else res[0]


def _rope(blk, c, s1, s2):
    return blk * c + pltpu.roll(blk, ROPE_HALF, 1) * s1 + pltpu.roll(blk, LANES - ROPE_HALF, 1) * s2


def _mixin_body(x_ref, g_ref, win_ref, qn_ref, wuq_ref, kvn_ref, wuk_ref, c_ref, s1_ref, s2_ref,
                q_ref, k_ref, ckv_ref, ckvb_ref, krp_ref, misc_ref, z_ref, xbc_ref, u_ref, v_ref):
    h = _rms(x_ref[...], g_ref[...]).astype(BF16)
    p = _dot(h, win_ref[...])
    z_ref[...] = p[:, P_Z:P_XBC]
    xbc_ref[...] = p[:, P_XBC:P_U]
    u_ref[...] = p[:, P_U:P_V]
    v_ref[...] = p[:, P_V:P_MISC]
    misc = p[:, P_MISC:P_END]
    misc_ref[...] = misc
    c, s1, s2 = c_ref[...], s1_ref[...], s2_ref[...]
    krp = _rope(misc, c, s1, s2)
    krp_ref[...] = krp
    ckvn = _rms(p[:, P_CKV:P_Z], kvn_ref[...])
    ckv_ref[...] = ckvn
    ckvb = ckvn.astype(BF16)
    ckvb_ref[...] = ckvb
    kn = _dot(ckvb, wuk_ref[...])
    qn = _rms(p[:, P_CQ:P_CKV], qn_ref[...]).astype(BF16)
    q = _dot(qn, wuq_ref[...]) * (MLA_SCALE * LOG2E)
    for hh in range(MLA_HEADS):
        sl = slice(hh * HEAD_PAD, (hh + 1) * HEAD_PAD)
        q_ref[hh] = _rope(q[:, sl], c, s1, s2).astype(BF16)
        k_ref[hh] = (kn[:, sl] + krp).astype(BF16)


def _mixin_tile(tp, ts, seq):
    return _pick(math.gcd(math.gcd(tp, ts), seq), (512, 256, 128, 64, 32, 16))


def _mixin(x, lw, tabs, tp, seq):
    t = x.shape[0]
    tm = _mixin_tile(tp, t - tp, seq)
    npb, nqt = tp // tm, seq // tm
    row = lambda i: (i, 0)
    trow = lambda i: (jnp.where(i < npb, i % nqt, nqt), 0)
    hrow = lambda i: (0, i, 0)
    wide = MLA_HEADS * HEAD_PAD
    outs = [
        ((MLA_HEADS, t, HEAD_PAD), BF16, pl.BlockSpec((MLA_HEADS, tm, HEAD_PAD), hrow)),
        ((MLA_HEADS, t, HEAD_PAD), BF16, pl.BlockSpec((MLA_HEADS, tm, HEAD_PAD), hrow)),
        ((t, KV_LORA), F32, pl.BlockSpec((tm, KV_LORA), row)),
        ((t, KV_LORA), BF16, pl.BlockSpec((tm, KV_LORA), row)),
        ((t, LANES), F32, pl.BlockSpec((tm, LANES), row)),
        ((t, LANES), F32, pl.BlockSpec((tm, LANES), row)),
        ((t, D_INNER), F32, pl.BlockSpec((tm, D_INNER), row)),
        ((t, CONV_DIM), F32, pl.BlockSpec((tm, CONV_DIM), row)),
        ((t, GM_WIDTH), F32, pl.BlockSpec((tm, GM_WIDTH), row)),
        ((t, GM_WIDTH), F32, pl.BlockSpec((tm, GM_WIDTH), row)),
    ]
    return pl.pallas_call(
        _mixin_body, grid=(t // tm,),
        in_specs=[pl.BlockSpec((tm, D_MODEL), row), _const_spec((1, D_MODEL)),
                  _resident_spec((D_MODEL, P_END)), _const_spec((1, Q_LORA)),
                  _resident_spec((Q_LORA, wide)), _const_spec((1, KV_LORA)),
                  _resident_spec((KV_LORA, wide)),
                  pl.BlockSpec((tm, LANES), trow), pl.BlockSpec((tm, LANES), trow),
                  pl.BlockSpec((tm, LANES), trow)],
        out_specs=[o[2] for o in outs],
        out_shape=[jax.ShapeDtypeStruct(o[0], o[1]) for o in outs],
        compiler_params=_params(("parallel",)), name="mixin",
    )(x, lw["mix_norm"], lw["w_in_p"], lw["q_norm"], lw["w_uq_p"], lw["kv_norm"], lw["w_uk_p"], *tabs)


def _pattn_body(qi_ref, kj_ref, q_ref, k_ref, v_ref, wuv_ref, o_ref, m_ref, l_ref, acc_ref, *, tq, kvb):
    n = pl.program_id(1)
    i = qi_ref[n]
    j = kj_ref[n]
    last = (i * tq + tq - 1) // kvb

    @pl.when(j == 0)
    def _():
        m_ref[...] = jnp.full(m_ref.shape, NEG_INF, F32)
        l_ref[...] = jnp.zeros(l_ref.shape, F32)
        acc_ref[...] = jnp.zeros(acc_ref.shape, F32)

    def step(masked):
        v = v_ref[...]
        if masked:
            qpos = i * tq + lax.broadcasted_iota(jnp.int32, (tq, kvb), 0)
            kpos = j * kvb + lax.broadcasted_iota(jnp.int32, (tq, kvb), 1)
            ok = kpos <= qpos
        for hh in range(MLA_HEADS):
            s = _dot_nt(q_ref[hh], k_ref[hh])
            if masked:
                s = jnp.where(ok, s, NEG_INF)
            m_prev = m_ref[hh]
            m_new = jnp.maximum(m_prev, jnp.max(s, axis=1, keepdims=True))
            alpha = jnp.exp2(m_prev - m_new)
            p = jnp.exp2(s - _lane_tile(m_new, kvb // LANES))
            psum = p[:, :LANES]
            for t in range(1, kvb // LANES):
                psum = psum + p[:, t * LANES:(t + 1) * LANES]
            l_ref[hh] = alpha * l_ref[hh] + psum
            acc_ref[hh] = acc_ref[hh] * _lane_tile(alpha, KV_LORA // LANES) + _dot(p.astype(BF16), v)
            m_ref[hh] = m_new

    @pl.when(j == last)
    def _():
        step(True)
        for hh in range(MLA_HEADS):
            o = (acc_ref[hh] / jnp.sum(l_ref[hh], axis=1, keepdims=True)).astype(BF16)
            o_ref[:, hh * V_HEAD:(hh + 1) * V_HEAD] = _dot(o, wuv_ref[hh]).astype(BF16)

    @pl.when(j != last)
    def _():
        step(False)


def _prompt_attention(q, k, vb, w_uv, bsz, seq):
    tq = _pick(seq, (512, 256, 128))
    kvb = _pick(seq, (512, 256, 128))
    nq, nkv = seq // tq, seq // kvb
    pairs = [(i, j) for i in range(nq) for j in range((i * tq + tq - 1) // kvb + 1)]
    qi = jnp.asarray(np.array([p[0] for p in pairs], np.int32))
    kj = jnp.asarray(np.array([p[1] for p in pairs], np.int32))
    grid_spec = pltpu.PrefetchScalarGridSpec(
        num_scalar_prefetch=2, grid=(bsz, len(pairs)),
        in_specs=[
            pl.BlockSpec((MLA_HEADS, tq, HEAD_PAD), lambda b, n, qi, kj: (0, b * nq + qi[n], 0)),
            pl.BlockSpec((MLA_HEADS, kvb, HEAD_PAD), lambda b, n, qi, kj: (0, b * nkv + kj[n], 0)),
            pl.BlockSpec((kvb, KV_LORA), lambda b, n, qi, kj: (b * nkv + kj[n], 0)),
            pl.BlockSpec((MLA_HEADS, KV_LORA, V_HEAD), lambda b, n, qi, kj: (0, 0, 0)),
        ],
        out_specs=pl.BlockSpec((tq, MLA_HEADS * V_HEAD), lambda b, n, qi, kj: (b * nq + qi[n], 0)),
        scratch_shapes=[pltpu.VMEM((MLA_HEADS, tq, LANES), F32), pltpu.VMEM((MLA_HEADS, tq, LANES), F32),
                        pltpu.VMEM((MLA_HEADS, tq, KV_LORA), F32)],
    )
    return pl.pallas_call(
        functools.partial(_pattn_body, tq=tq, kvb=kvb), grid_spec=grid_spec,
        out_shape=jax.ShapeDtypeStruct((bsz * seq, MLA_HEADS * V_HEAD), BF16),
        compiler_params=_params(("parallel", "arbitrary")), name="prompt_attn",
    )(qi, kj, q, k, vb, w_uv)


def _sattn_body(pt_ref, q_ref, ckvn_ref, krn_ref, wukt_ref, wuv_ref, ckv_hbm, ckr_hbm, o_ref,
                kbuf, rbuf, sem, qlat_ref, qrope_ref, m_ref, l_ref, acc_ref, *, layer, pc, n_seq, n_ch, page, nsub):
    b = pl.program_id(0)
    c = pl.program_id(1)
    step = b * n_ch + c
    slot = step % 2
    n_new = q_ref.shape[1] // MLA_HEADS

    def copies(bb, cc, sl):
        kv, kr = [], []
        for pg in range(pc):
            pid = pt_ref[bb, cc * pc + pg]
            kv.append(pltpu.make_async_copy(ckv_hbm.at[layer, pid], kbuf.at[sl, pl.ds(pg * page, page)], sem.at[0, sl]))
            kr.append(pltpu.make_async_copy(ckr_hbm.at[layer, pid], rbuf.at[sl, :, pl.ds(pg * page, page)], sem.at[1, sl]))
        return kv + kr

    @pl.when(step == 0)
    def _():
        for cp in copies(b, c, slot):
            cp.start()

    @pl.when(step + 1 < n_seq * n_ch)
    def _():
        wrap = c + 1 == n_ch
        nb = jnp.where(wrap, b + 1, b)
        nc = jnp.where(wrap, 0, c + 1)
        for cp in copies(nb, nc, 1 - slot):
            cp.start()

    @pl.when(c == 0)
    def _():
        m_ref[...] = jnp.full(m_ref.shape, NEG_INF, F32)
        l_ref[...] = jnp.zeros(l_ref.shape, F32)
        acc_ref[...] = jnp.zeros(acc_ref.shape, F32)
        q32 = q_ref[0].astype(F32)
        for hh in range(MLA_HEADS):
            qh = q32[hh * n_new:(hh + 1) * n_new, :].astype(BF16)
            qlat_ref[hh * n_new:(hh + 1) * n_new, :] = _dot(qh, wukt_ref[hh])
        qrope_ref[...] = q32[:, ROPE_LO:ROPE_LO + QK_ROPE]

    def part(s, vals):
        m = jnp.max(s, axis=1, keepdims=True)
        p = jnp.exp2(s - m)
        return m, jnp.sum(p, axis=1, keepdims=True), _dot(p.astype(BF16), vals)

    def merge(parts):
        m_prev = m_ref[...]
        m_new = m_prev
        for m, _, _ in parts:
            m_new = jnp.maximum(m_new, m)
        alpha = jnp.exp2(m_prev - m_new)
        l_new = alpha * l_ref[...]
        acc = acc_ref[...] * alpha
        for m, l, a in parts:
            w = jnp.exp2(m - m_new)
            l_new = l_new + w * l
            acc = acc + w * a
        m_ref[...] = m_new
        l_ref[...] = l_new
        acc_ref[...] = acc

    for cp in copies(b, c, slot):
        cp.wait()
    qlat = qlat_ref[...].astype(BF16)
    qrope = qrope_ref[...].astype(BF16)
    keys = pc * page // nsub
    parts = []
    for u in range(nsub):
        kc = kbuf[slot, u * keys:(u + 1) * keys, :].astype(BF16)
        rc = rbuf[slot, :, u * keys:(u + 1) * keys].astype(BF16)
        parts.append(part(_dot_nt(qlat, kc) + _dot(qrope, rc), kc))
    merge(parts)

    @pl.when(c == n_ch - 1)
    def _():
        pad = LANES - n_new
        kn = jnp.concatenate([ckvn_ref[...], jnp.zeros((pad, KV_LORA), F32)], axis=0).astype(BF16)
        rn = jnp.concatenate([krn_ref[...][:, ROPE_LO:ROPE_LO + QK_ROPE], jnp.zeros((pad, QK_ROPE), F32)],
                             axis=0).astype(BF16)
        s = _dot_nt(qlat, kn) + _dot_nt(qrope, rn)
        tpos = lax.broadcasted_iota(jnp.int32, s.shape, 0) % n_new
        kpos = lax.broadcasted_iota(jnp.int32, s.shape, 1)
        merge([part(jnp.where(kpos <= tpos, s, NEG_INF), kn)])
        o = acc_ref[...] / l_ref[...]
        pieces = [_dot(o[hh * n_new:(hh + 1) * n_new, :].astype(BF16), wuv_ref[hh]) for hh in range(MLA_HEADS)]
        o_ref[0] = jnp.concatenate(pieces, axis=1).astype(BF16)


SATTN_PAGES = 64
SATTN_SPLIT = 2


def _sample_attention(q_s, ckv_all, krp_all, w_ukt_p, w_uv, cache_kv, cache_kr_t, page_table, layer, tp, n_new):
    n_seq, n_pages = page_table.shape
    page = cache_kv.shape[2]
    pc = _pick(n_pages, (SATTN_PAGES, 16, 8, 4, 2, 1))
    nsub = SATTN_SPLIT if pc % SATTN_SPLIT == 0 else 1
    n_ch = n_pages // pc
    rows = MLA_HEADS * n_new
    off = tp // n_new
    grid_spec = pltpu.PrefetchScalarGridSpec(
        num_scalar_prefetch=1, grid=(n_seq, n_ch),
        in_specs=[
            pl.BlockSpec((1, rows, HEAD_PAD), lambda b, c, pt: (b, 0, 0)),
            pl.BlockSpec((n_new, KV_LORA), lambda b, c, pt: (off + b, 0)),
            pl.BlockSpec((n_new, LANES), lambda b, c, pt: (off + b, 0)),
            pl.BlockSpec((MLA_HEADS, HEAD_PAD, KV_LORA), lambda b, c, pt: (0, 0, 0)),
            pl.BlockSpec((MLA_HEADS, KV_LORA, V_HEAD), lambda b, c, pt: (0, 0, 0)),
            pl.BlockSpec(memory_space=pl.ANY),
            pl.BlockSpec(memory_space=pl.ANY),
        ],
        out_specs=pl.BlockSpec((1, n_new, MLA_HEADS * V_HEAD), lambda b, c, pt: (b, 0, 0)),
        scratch_shapes=[
            pltpu.VMEM((2, pc * page, KV_LORA), F32), pltpu.VMEM((2, QK_ROPE, pc * page), F32),
            pltpu.SemaphoreType.DMA((2, 2)),
            pltpu.VMEM((rows, KV_LORA), F32), pltpu.VMEM((rows, QK_ROPE), F32),
            pltpu.VMEM((rows, 1), F32), pltpu.VMEM((rows, 1), F32), pltpu.VMEM((rows, KV_LORA), F32),
        ],
    )
    out = pl.pallas_call(
        functools.partial(_sattn_body, layer=layer, pc=pc, n_seq=n_seq, n_ch=n_ch, page=page, nsub=nsub),
        grid_spec=grid_spec,
        out_shape=jax.ShapeDtypeStruct((n_seq, n_new, MLA_HEADS * V_HEAD), BF16),
        compiler_params=_params(("arbitrary", "arbitrary")), name="sample_attn",
    )(page_table, q_s, ckv_all, krp_all, w_ukt_p, w_uv, cache_kv, cache_kr_t)
    return out.reshape(n_seq * n_new, MLA_HEADS * V_HEAD)


def _ssd_consts(seg):
    r = np.arange(CHUNK)
    same = (r[:, None] // seg) == (r[None, :] // seg)
    caus = (same & (r[None, :] <= r[:, None])).astype(np.float32)
    plast = (r[None, :] == (r[:, None] // seg) * seg + seg - 1).astype(np.float32)
    et = np.zeros((D_INNER, LANES), np.float32)
    et[np.arange(D_INNER), MISC_DT + np.arange(D_INNER) // SSM_HEAD_DIM] = 1.0
    return jnp.asarray(caus), jnp.asarray(plast), jnp.asarray(et)


def _ssd_tile(act, z, misc, caus, plast, et, dtb, alog, dskip, ng, h_in_fn, seg):
    nseg = CHUNK // seg
    lane = lax.broadcasted_iota(jnp.int32, (1, LANES), 1)
    hmask = (lane >= MISC_DT) & (lane < MISC_DT + SSM_HEADS)
    lo = lax.broadcasted_iota(jnp.int32, (CHUNK, LANES), 1) < SSM_HEAD_DIM
    xs = act[:, :D_INNER]
    bm = act[:, D_INNER:D_INNER + LANES]
    cm = act[:, D_INNER + LANES:D_INNER + 2 * LANES]
    dtv = jnp.where(hmask, jax.nn.softplus(misc + dtb), 0.0)
    a_row = jnp.where(hmask, -jnp.exp(alog), 0.0)
    acs = _dot_exact(caus, dtv * a_row)
    acs_last = _dot_exact(plast, acs)
    e_acs = jnp.exp(acs)
    to_end = jnp.exp(acs_last - acs)
    e_last = jnp.exp(acs_last)
    acs_t = acs.T
    dec_all = _dot_exact(et, e_last, NT_DIMS)

    bm_b = bm.astype(BF16)
    cm_g = [jnp.where(lo, cm, 0.0).astype(BF16), jnp.where(lo, 0.0, cm).astype(BF16)]
    g_mat = [_dot_nt(cm_g[g], bm_b) for g in range(SSM_GROUPS)]
    cm_half = [cm[:, :D_STATE].astype(BF16), cm[:, D_STATE:].astype(BF16)]

    def col(arr, hh):
        return arr[:, MISC_DT + hh:MISC_DT + hh + 1]

    def pair(arr, pb):
        return jnp.where(lo, col(arr, 2 * pb), col(arr, 2 * pb + 1))

    hpg = SSM_HEADS // SSM_GROUPS
    rows_g = hpg * SSM_HEAD_DIM
    y_off = []
    for g in range(SSM_GROUPS):
        parts = []
        for j in range(nseg):
            hj = h_in_fn(j)[g * rows_g:(g + 1) * rows_g, :].astype(BF16)
            parts.append(_dot_nt(cm_half[g][j * seg:(j + 1) * seg, :], hj))
        y_off.append(parts[0] if nseg == 1 else jnp.concatenate(parts, axis=0))

    y_blocks, xte_blocks = [], []
    for pb in range(SSM_HEADS // 2):
        g = (2 * pb) // hpg
        sl = slice(pb * LANES, (pb + 1) * LANES)
        xs_p = xs[:, sl]
        xdt = xs_p * pair(dtv, pb)
        xte_blocks.append((xdt * pair(to_end, pb)).astype(BF16))
        xdt_b = xdt.astype(BF16)
        yd = []
        for hh in (2 * pb, 2 * pb + 1):
            segm = col(acs, hh) - acs_t[MISC_DT + hh:MISC_DT + hh + 1, :]
            decay = jnp.exp(jnp.where(caus > 0.0, segm, NEG_INF))
            yd.append(_dot((g_mat[g] * decay).astype(BF16), xdt_b))
        y_d = jnp.where(lo, yd[0], yd[1])
        off_sl = slice((pb % (hpg // 2)) * LANES, (pb % (hpg // 2) + 1) * LANES)
        y_o = y_off[g][:, off_sl] * pair(e_acs, pb)
        y_blocks.append(y_d + y_o + dskip[:, sl] * xs_p)
    y = jnp.concatenate(y_blocks, axis=1)
    yn = _rms(y * _silu(z), ng)

    xte = jnp.concatenate(xte_blocks, axis=1)
    rowi = lax.broadcasted_iota(jnp.int32, (D_INNER, D_STATE), 0) < rows_g
    new_states = []
    if nseg == 1:
        s_all = lax.dot_general(xte, bm_b, TN_DIMS, preferred_element_type=F32)
        s_sel = jnp.where(rowi, s_all[:, :D_STATE], s_all[:, D_STATE:])
        new_states.append(h_in_fn(0) * dec_all[:, 0:1] + s_sel)
    else:
        tr = lax.broadcasted_iota(jnp.int32, (CHUNK, LANES), 0) // seg
        bbd = jnp.concatenate([jnp.where(tr == j, bm, 0.0) for j in range(nseg)], axis=1).astype(BF16)
        s_all = lax.dot_general(xte, bbd, TN_DIMS, preferred_element_type=F32)
        for j in range(nseg):
            sj = s_all[:, j * LANES:(j + 1) * LANES]
            s_sel = jnp.where(rowi, sj[:, :D_STATE], sj[:, D_STATE:])
            new_states.append(h_in_fn(j) * dec_all[:, j * seg:j * seg + 1] + s_sel)
    return yn, new_states


def _ssd_prompt_body(*refs, bsz):
    zs, xbcs, miscs = refs[:bsz], refs[bsz:2 * bsz], refs[2 * bsz:3 * bsz]
    (caus_ref, plast_ref, et_ref, cw_ref, cb_ref, dtb_ref, alog_ref, dskip_ref, ng_ref,
     y_ref, st_ref, xpad_ref) = refs[3 * bsz:]
    c = pl.program_id(0)
    hist = CONV_K - 1

    @pl.when(c == 0)
    def _():
        xpad_ref[:, 0:8, :] = jnp.zeros((bsz, 8, CONV_DIM), F32)
        st_ref[...] = jnp.zeros(st_ref.shape, F32)

    for b in range(bsz):
        x = xbcs[b][...]
        xpad_ref[b, 8:8 + CHUNK, :] = x
        conv = cb_ref[...] + x * cw_ref[hist:hist + 1, :]
        for k in range(hist):
            conv = conv + xpad_ref[b, 8 - hist + k:8 - hist + k + CHUNK, :] * cw_ref[k:k + 1, :]
        xpad_ref[b, 0:8, :] = x[CHUNK - 8:, :]
        act = _silu(conv)
        yn, new = _ssd_tile(act, zs[b][...], miscs[b][...], caus_ref[...], plast_ref[...], et_ref[...],
                            dtb_ref[...], alog_ref[...], dskip_ref[...], ng_ref[...], lambda j, b=b: st_ref[b], CHUNK)
        y_ref[b] = yn.astype(BF16)
        st_ref[b] = new[0]


def _ssd_sample_body(z_ref, xbc_ref, misc_ref, sp_ref, h0_ref, caus_ref, plast_ref, et_ref, cw_ref, cb_ref,
                     dtb_ref, alog_ref, dskip_ref, ng_ref, y_ref, st_ref, *, seg):
    hist = CONV_K - 1
    x = xbc_ref[...]
    sp = sp_ref[...]
    tpos = lax.broadcasted_iota(jnp.int32, (CHUNK, CONV_DIM), 0) % seg
    conv = cb_ref[...] + x * cw_ref[hist:hist + 1, :]
    for j in range(1, CONV_K):
        xin = pltpu.roll(x, j, 0)
        up = hist - j
        sin = sp if up == 0 else pltpu.roll(sp, CHUNK - up, 0)
        conv = conv + jnp.where(tpos >= j, xin, sin) * cw_ref[hist - j:hist - j + 1, :]
    act = _silu(conv)
    yn, new = _ssd_tile(act, z_ref[...], misc_ref[...], caus_ref[...], plast_ref[...], et_ref[...],
                        dtb_ref[...], alog_ref[...], dskip_ref[...], ng_ref[...], lambda j: h0_ref[j], seg)
    y_ref[...] = yn.astype(BF16)
    for j in range(len(new)):
        st_ref[j] = new[j]


def _ssd_param_args(lw):
    return [lw["conv_w"], lw["conv_b"], lw["dtb_row"], lw["alog_row"], lw["dskip_exp"], lw["ssm_norm"]]


def _ssd_param_specs():
    return [_const_spec((CONV_K, CONV_DIM)), _const_spec((1, CONV_DIM)), _const_spec((1, LANES)),
            _const_spec((1, LANES)), _const_spec((1, D_INNER)), _const_spec((1, D_INNER))]


def _ssd_prompt(z, xbc, misc, lw, bsz, seq):
    nc = seq // CHUNK
    caus, plast, et = _ssd_consts(CHUNK)
    rows = [(lambda c, b=b: (b * nc + c, 0)) for b in range(bsz)]
    y, st = pl.pallas_call(
        functools.partial(_ssd_prompt_body, bsz=bsz), grid=(nc,),
        in_specs=[pl.BlockSpec((CHUNK, D_INNER), r) for r in rows]
        + [pl.BlockSpec((CHUNK, CONV_DIM), r) for r in rows] + [pl.BlockSpec((CHUNK, LANES), r) for r in rows]
        + [_const_spec((CHUNK, CHUNK)), _const_spec((CHUNK, CHUNK)), _const_spec((D_INNER, LANES))]
        + _ssd_param_specs(),
        out_specs=[pl.BlockSpec((bsz, CHUNK, D_INNER), lambda c: (0, c, 0)),
                   pl.BlockSpec((bsz, D_INNER, D_STATE), lambda c: (0, 0, 0))],
        out_shape=[jax.ShapeDtypeStruct((bsz, seq, D_INNER), BF16),
                   jax.ShapeDtypeStruct((bsz, D_INNER, D_STATE), F32)],
        scratch_shapes=[pltpu.VMEM((bsz, 8 + CHUNK, CONV_DIM), F32)],
        compiler_params=_params(("arbitrary",)), name="ssd_prompt",
    )(*([z] * bsz), *([xbc] * bsz), *([misc] * bsz), caus, plast, et, *_ssd_param_args(lw))
    return y.reshape(bsz * seq, D_INNER), st


def _ssd_sample(z, xbc, misc, sp, h0, lw, tp, n_seq, seg):
    nseg = CHUNK // seg
    nt = n_seq // nseg
    off = tp // CHUNK
    caus, plast, et = _ssd_consts(seg)
    row = lambda i: (off + i, 0)
    y, st = pl.pallas_call(
        functools.partial(_ssd_sample_body, seg=seg), grid=(nt,),
        in_specs=[pl.BlockSpec((CHUNK, D_INNER), row), pl.BlockSpec((CHUNK, CONV_DIM), row),
                  pl.BlockSpec((CHUNK, LANES), row), pl.BlockSpec((CHUNK, CONV_DIM), lambda i: (i, 0)),
                  pl.BlockSpec((nseg, D_INNER, D_STATE), lambda i: (i, 0, 0)),
                  _const_spec((CHUNK, CHUNK)), _const_spec((CHUNK, CHUNK)),
                  _const_spec((D_INNER, LANES))] + _ssd_param_specs(),
        out_specs=[pl.BlockSpec((CHUNK, D_INNER), lambda i: (i, 0)),
                   pl.BlockSpec((nseg, D_INNER, D_STATE), lambda i: (i, 0, 0))],
        out_shape=[jax.ShapeDtypeStruct((n_seq * seg, D_INNER), BF16),
                   jax.ShapeDtypeStruct((n_seq, D_INNER, D_STATE), F32)],
        compiler_params=_params(("parallel",)), name="ssd_sample",
    )(z, xbc, misc, sp, h0, caus, plast, et, *_ssd_param_args(lw))
    return y, st


def _gmlp_body(u_ref, v_ref, w_ref, mask_ref, b_ref, ng_ref, o_ref, *vn_out, reps):
    lo = lax.broadcasted_iota(jnp.int32, (CHUNK, LANES), 1) < GM_GROUP_DIM
    mask = mask_ref[...] > 0.0
    wm = [jnp.where(mask, w_ref[g], 0.0).astype(BF16) for g in range(GM_GROUPS)]
    bias = b_ref[...]
    for r in range(reps):
        rs = slice(r * CHUNK, (r + 1) * CHUNK)
        ug = jax.nn.gelu(u_ref[rs, :])
        vn = _rms(jax.nn.gelu(v_ref[rs, :]), ng_ref[...])
        if vn_out:
            vn_out[0][rs, :] = vn
        vb = vn.astype(BF16)
        for pb in range(GM_GROUPS // 2):
            sl = slice(pb * LANES, (pb + 1) * LANES)
            s = jnp.where(lo, _dot(wm[2 * pb], vb[:, sl]), _dot(wm[2 * pb + 1], vb[:, sl])) + bias[:, sl]
            o_ref[rs, sl] = (ug[:, sl] * s).astype(BF16)


def _gmlp(u, v, w_sp, mask, bias_exp, norm_g, row_off, n_rows, want_vn):
    reps = _pick(n_rows // CHUNK, (4, 2, 1))
    tm = reps * CHUNK
    off = row_off // tm
    row = lambda i: (off + i, 0)
    out_shape = [jax.ShapeDtypeStruct((n_rows, GM_WIDTH), BF16)]
    out_specs = [pl.BlockSpec((tm, GM_WIDTH), lambda i: (i, 0))]
    if want_vn:
        out_shape.append(jax.ShapeDtypeStruct((n_rows, GM_WIDTH), F32))
        out_specs.append(pl.BlockSpec((tm, GM_WIDTH), lambda i: (i, 0)))
    return pl.pallas_call(
        functools.partial(_gmlp_body, reps=reps), grid=(n_rows // tm,),
        in_specs=[pl.BlockSpec((tm, GM_WIDTH), row), pl.BlockSpec((tm, GM_WIDTH), row),
                  _const_spec((GM_GROUPS, CHUNK, CHUNK)), _const_spec((CHUNK, CHUNK)),
                  _const_spec((CHUNK, GM_WIDTH)), _const_spec((1, GM_WIDTH))],
        out_specs=out_specs, out_shape=out_shape,
        compiler_params=_params(("parallel",)), name="gmlp",
    )(u, v, w_sp, mask, bias_exp, norm_g)


def _merge_body(x_ref, ohp_ref, ysp_ref, gmp_ref, ohs_ref, yss_ref, gms_ref, ng_ref, wg_ref, bg_ref, womla_ref,
                wossm_ref, wogm_ref, wout_ref, mqn_ref, wmq_ref, xo_ref, qm_ref, *, npb):
    def compute(oh_ref, ys_ref, gm_ref):
        x = x_ref[...]
        h = _rms(x, ng_ref[...]).astype(BF16)
        gates = _dot(h, wg_ref[...]) + bg_ref[...]
        gates = 1.0 / (1.0 + jnp.exp(-gates))
        mix = (gates[:, :D_MODEL] * _dot(oh_ref[...], womla_ref[...])
               + gates[:, D_MODEL:2 * D_MODEL] * _dot(ys_ref[...], wossm_ref[...])
               + gates[:, 2 * D_MODEL:] * _dot(gm_ref[...], wogm_ref[...]))
        x2 = x + _dot(mix.astype(BF16), wout_ref[...])
        xo_ref[...] = x2
        hq = _rms(x2, mqn_ref[...]).astype(BF16)
        qm_ref[...] = _dot(hq, wmq_ref[...])

    @pl.when(pl.program_id(0) < npb)
    def _():
        compute(ohp_ref, ysp_ref, gmp_ref)

    @pl.when(pl.program_id(0) >= npb)
    def _():
        compute(ohs_ref, yss_ref, gms_ref)


def _group_row_maps(npb):
    return (lambda i: (jnp.minimum(i, npb - 1), 0)), (lambda i: (jnp.maximum(i - npb, 0), 0))


def _merge(x, branches_p, branches_s, lw, tp):
    t = x.shape[0]
    tm = _pick(math.gcd(tp, t - tp), (512, 256, 128, 64, 32, 16))
    row = lambda i: (i, 0)
    prow, srow = _group_row_maps(tp // tm)
    half = MLA_HEADS * V_HEAD
    widths = (half, D_INNER, GM_WIDTH)
    return pl.pallas_call(
        functools.partial(_merge_body, npb=tp // tm), grid=(t // tm,),
        in_specs=[pl.BlockSpec((tm, D_MODEL), row)]
        + [pl.BlockSpec((tm, w), prow) for w in widths] + [pl.BlockSpec((tm, w), srow) for w in widths]
        + [_const_spec((1, D_MODEL)), _resident_spec((D_MODEL, 3 * D_MODEL)), _const_spec((1, 3 * D_MODEL)),
           _resident_spec((half, D_MODEL)), _resident_spec((D_INNER, D_MODEL)),
           _resident_spec((GM_WIDTH, D_MODEL)), _resident_spec((D_MODEL, D_MODEL)),
           _const_spec((1, D_MODEL)), _resident_spec((D_MODEL, MEM_WIDTH))],
        out_specs=[pl.BlockSpec((tm, D_MODEL), row), pl.BlockSpec((tm, MEM_WIDTH), row)],
        out_shape=[jax.ShapeDtypeStruct((t, D_MODEL), F32), jax.ShapeDtypeStruct((t, MEM_WIDTH), F32)],
        input_output_aliases={0: 0}, compiler_params=_params(("parallel",)), name="merge",
    )(x, *branches_p, *branches_s, lw["mix_norm"], lw["w_gate"], lw["b_gate"], lw["w_o_mla"], lw["w_o_ssm"],
      lw["w_o_gm"], lw["w_out"], lw["mem_q_norm"], lw["w_mem_q"])


def _memattn_body(q_ref, k_ref, v_ref, o_ref, *, sb, feat_major):
    q = q_ref[...]
    rows = q.shape[0] // sb
    if feat_major:
        m = k_ref.shape[2]
        kb = jnp.concatenate([k_ref[e] for e in range(sb)], axis=1).astype(BF16)
        vb = jnp.concatenate([v_ref[e] for e in range(sb)], axis=1).astype(BF16)
    else:
        m = k_ref.shape[1]
        kb = k_ref[...].reshape(sb * m, MEM_WIDTH).astype(BF16)
        vb = v_ref[...].reshape(sb * m, MEM_WIDTH).astype(BF16)
    lane = lax.broadcasted_iota(jnp.int32, q.shape, 1) // MEM_HEAD_DIM
    if sb > 1:
        own = (lax.broadcasted_iota(jnp.int32, (q.shape[0], sb * m), 0) // rows
               == lax.broadcasted_iota(jnp.int32, (q.shape[0], sb * m), 1) // m)
    out = jnp.zeros(q.shape, F32)
    for hh in range(MEM_HEADS):
        sel = lane == hh
        qh = jnp.where(sel, q, 0.0).astype(BF16)
        s = (_dot(qh, kb) if feat_major else _dot_nt(qh, kb)) * MEM_SCALE
        if sb > 1:
            s = jnp.where(own, s, NEG_INF)
        p = jnp.exp(s - jnp.max(s, axis=1, keepdims=True))
        p = (p / jnp.sum(p, axis=1, keepdims=True)).astype(BF16)
        out = out + jnp.where(sel, _dot_nt(p, vb) if feat_major else _dot(p, vb), 0.0)
    o_ref[...] = out


MEMATTN_BATCH = 8


def _mem_attention(qm, mem_k, mem_v, row_off, nb, rows_per_b, layer=None):
    tm = _pick(rows_per_b, (512, 256, 128, 64, 32, 16, 8))
    nt = rows_per_b // tm
    sb = MEMATTN_BATCH if (nt == 1 and nb % MEMATTN_BATCH == 0 and tm * MEMATTN_BATCH <= 512) else 1
    off = row_off // (tm * sb)
    if layer is None:
        kv_spec = pl.BlockSpec((sb,) + mem_k.shape[1:], lambda b, i: (b, 0, 0))
    else:
        kv_spec = pl.BlockSpec((None, sb) + mem_k.shape[2:], lambda b, i: (layer, b, 0, 0))
    return pl.pallas_call(
        functools.partial(_memattn_body, sb=sb, feat_major=layer is not None), grid=(nb // sb, nt),
        in_specs=[pl.BlockSpec((tm * sb, MEM_WIDTH), lambda b, i: (off + b * nt + i, 0)), kv_spec, kv_spec],
        out_specs=pl.BlockSpec((tm * sb, MEM_WIDTH), lambda b, i: (b * nt + i, 0)),
        out_shape=jax.ShapeDtypeStruct((nb * rows_per_b, MEM_WIDTH), F32),
        compiler_params=_params(("parallel", "parallel")), name="mem_attn",
    )(qm, mem_k, mem_v)


def _memkv_body(m_ref, g_ref, w_ref, o_ref):
    o_ref[...] = _dot(_rms(m_ref[...], g_ref[...]).astype(BF16), w_ref[...])


def _memory_kv(mem2d, norm_g, w_kv):
    t = mem2d.shape[0]
    tm = _pick(t, (256, 128, 64, 32, 16, 8))
    return pl.pallas_call(
        _memkv_body, grid=(t // tm,),
        in_specs=[pl.BlockSpec((tm, D_MODEL), lambda i: (i, 0)), _const_spec((1, D_MODEL)),
                  _const_spec((D_MODEL, 2 * MEM_WIDTH))],
        out_specs=pl.BlockSpec((tm, 2 * MEM_WIDTH), lambda i: (i, 0)),
        out_shape=jax.ShapeDtypeStruct((t, 2 * MEM_WIDTH), F32),
        compiler_params=_params(("parallel",)), name="mem_kv",
    )(mem2d, norm_g, w_kv)


def _layer_weights(l, p, seg):
    w_in = p["w_in"][l]
    offs = np.cumsum((0,) + IN_SIZES)
    cq, ckv, kr, z, xbc, dt, u, v = [w_in[:, offs[i]:offs[i + 1]] for i in range(len(IN_SIZES))]
    zc = lambda n: jnp.zeros((D_MODEL, n), F32)
    w_in_p = jnp.concatenate(
        [cq, ckv, z, xbc, u, v, zc(ROPE_LO), kr, dt, zc(LANES - ROPE_LO - QK_ROPE - SSM_HEADS)], axis=1)
    hd = QK_NOPE + QK_ROPE
    w_uq_p = jnp.pad(p["w_uq"][l].reshape(Q_LORA, MLA_HEADS, hd), ((0, 0), (0, 0), (0, HEAD_PAD - hd)))
    w_uk = p["w_uk"][l]
    w_uk_p = jnp.pad(jnp.transpose(w_uk, (1, 0, 2)), ((0, 0), (0, 0), (0, HEAD_PAD - QK_NOPE)))
    w_ukt_p = jnp.pad(jnp.transpose(w_uk, (0, 2, 1)), ((0, 0), (0, HEAD_PAD - QK_NOPE), (0, 0)))
    lane_pad = lambda a: jnp.pad(a.reshape(1, SSM_HEADS), ((0, 0), (MISC_DT, LANES - MISC_DT - SSM_HEADS)))
    w_sp = p["w_spatial"][l]
    b_sp = p["b_spatial"][l]
    bias_p = jnp.repeat(b_sp.T, GM_GROUP_DIM, axis=1)
    reps = CHUNK // seg
    w_sp_s = jnp.tile(w_sp[:, :seg, :seg], (1, reps, reps))
    bias_s = jnp.tile(jnp.repeat(b_sp[:, :seg].T, GM_GROUP_DIM, axis=1), (reps, 1))
    row = lambda a: a.reshape(1, -1)
    return {
        "ffn_norm": p["ffn_norm"][l],
        "mix_norm": row(p["mix_norm"][l]), "w_in_p": w_in_p.astype(BF16),
        "q_norm": row(p["q_norm"][l]), "w_uq_p": w_uq_p.reshape(Q_LORA, MLA_HEADS * HEAD_PAD).astype(BF16),
        "kv_norm": row(p["kv_norm"][l]), "w_uk_p": w_uk_p.reshape(KV_LORA, MLA_HEADS * HEAD_PAD).astype(BF16),
        "w_ukt_p": w_ukt_p.astype(BF16), "w_uv": p["w_uv"][l].astype(BF16),
        "w_o_mla": p["w_o_mla"][l].astype(BF16),
        "conv_w": p["conv_w"][l], "conv_b": row(p["conv_b"][l]),
        "dtb_row": lane_pad(p["dt_bias"][l]), "alog_row": lane_pad(p["a_log"][l]),
        "dskip_exp": row(jnp.repeat(p["d_skip"][l], SSM_HEAD_DIM)), "ssm_norm": row(p["ssm_norm"][l]),
        "w_o_ssm": p["w_o_ssm"][l].astype(BF16),
        "gm_norm": row(p["gm_norm"][l]), "w_sp_p": w_sp, "bias_p": bias_p, "w_sp_s": w_sp_s, "bias_s": bias_s,
        "w_o_gm": p["w_o_gm"][l].astype(BF16),
        "w_gate": p["w_gate"][l].astype(BF16), "b_gate": row(p["b_gate"][l]), "w_out": p["w_out"][l].astype(BF16),
        "mem_q_norm": row(p["mem_q_norm"][l]), "mem_kv_norm": row(p["mem_kv_norm"][l]),
        "w_mem_q": p["w_mem_q"][l].astype(BF16), "w_mem_kv": p["w_mem_kv"][l].astype(BF16),
        "w_mem_o": p["w_mem_o"][l].astype(BF16),
    }


def _rope_tables(pos):
    inv = ROPE_THETA ** (-jnp.arange(ROPE_HALF, dtype=F32) / ROPE_HALF)
    ang = pos.astype(F32)[:, None] * inv[None, :]
    cos, sin = jnp.cos(ang), jnp.sin(ang)
    n = pos.shape[0]
    zero = lambda w: jnp.zeros((n, w), F32)
    tail = LANES - ROPE_LO - QK_ROPE
    c = jnp.concatenate([jnp.ones((n, ROPE_LO), F32), cos, cos, zero(tail)], axis=1)
    s1 = jnp.concatenate([zero(ROPE_LO + ROPE_HALF), sin, zero(tail)], axis=1)
    s2 = jnp.concatenate([zero(ROPE_LO), -sin, zero(ROPE_HALF + tail)], axis=1)
    return c, s1, s2


def _spatial_masks(seg):
    r = np.arange(CHUNK)
    same = (r[:, None] // seg) == (r[None, :] // seg)
    return jnp.asarray((same & (r[None, :] <= r[:, None])).astype(np.float32))


def kernel(x_prompt, x_sample, mem_prompt, cache_kv_latent, cache_k_rope, state_ssm, state_conv, cache_mem_k, cache_mem_v, page_table, ffn_norm, w_ffn_in, w_ffn_out, mix_norm, w_in, q_norm, w_uq, kv_norm, w_uk, w_uv, w_o_mla, conv_w, conv_b, dt_bias, a_log, d_skip, ssm_norm, w_o_ssm, gm_norm, w_spatial, b_spatial, w_o_gm, w_gate, b_gate, w_out, mem_q_norm, mem_kv_norm, w_mem_q, w_mem_kv, w_mem_o, final_norm):
    params = dict(ffn_norm=ffn_norm, w_ffn_in=w_ffn_in, w_ffn_out=w_ffn_out, mix_norm=mix_norm, w_in=w_in,
                  q_norm=q_norm, w_uq=w_uq, kv_norm=kv_norm, w_uk=w_uk, w_uv=w_uv, w_o_mla=w_o_mla,
                  conv_w=conv_w, conv_b=conv_b, dt_bias=dt_bias, a_log=a_log, d_skip=d_skip, ssm_norm=ssm_norm,
                  w_o_ssm=w_o_ssm, gm_norm=gm_norm, w_spatial=w_spatial, b_spatial=b_spatial, w_o_gm=w_o_gm,
                  w_gate=w_gate, b_gate=b_gate, w_out=w_out, mem_q_norm=mem_q_norm, mem_kv_norm=mem_kv_norm,
                  w_mem_q=w_mem_q, w_mem_kv=w_mem_kv, w_mem_o=w_mem_o)
    bsz, seq, _ = x_prompt.shape
    n_seq, n_new, _ = x_sample.shape
    depth = w_in.shape[0]
    past = page_table.shape[1] * cache_kv_latent.shape[2]
    tp, ts = bsz * seq, n_seq * n_new
    mem_tokens = mem_prompt.shape[1]
    assert seq % CHUNK == 0 and CHUNK % n_new == 0 and ts % CHUNK == 0 and n_new % 8 == 0

    tab_tile = _mixin_tile(tp, ts, seq)
    pos = jnp.concatenate([jnp.arange(seq, dtype=jnp.int32),
                           jnp.tile(past + jnp.arange(n_new, dtype=jnp.int32), tab_tile // n_new)])
    tabs = _rope_tables(pos)
    mask_p, mask_s = _spatial_masks(CHUNK), _spatial_masks(n_new)
    hist = CONV_K - 1

    mem2d = mem_prompt.reshape(bsz * mem_tokens, D_MODEL)
    cache_kr_t = jnp.swapaxes(cache_k_rope, 2, 3)
    mem_k_t = jnp.transpose(cache_mem_k, (0, 1, 3, 4, 2)).reshape(depth, n_seq, MEM_WIDTH, -1)
    mem_v_t = jnp.transpose(cache_mem_v, (0, 1, 3, 4, 2)).reshape(depth, n_seq, MEM_WIDTH, -1)
    w_fi, w_fo = w_ffn_in.astype(BF16), w_ffn_out.astype(BF16)
    outs = {k: [] for k in ("kvl_p", "kr_p", "ssm_p", "conv_p", "mk_p", "mv_p", "kvl_s", "kr_s", "ssm_s", "conv_s", "gv_s")}
    lws = [_layer_weights(l, params, n_new) for l in range(depth)]
    x = _ffn((x_prompt.reshape(tp, D_MODEL), x_sample.reshape(ts, D_MODEL)), ffn_norm[0, 0], w_fi, w_fo, 0, 0, tp)
    y_p = y_s = None
    for l in range(depth):
        lw = lws[l]
        q, k, ckv, ckvb, krp, misc, z, xbc, u, v = _mixin(x, lw, tabs, tp, seq)

        oh_p = _prompt_attention(q, k, ckvb, lw["w_uv"], bsz, seq)
        q_s = jnp.transpose(q[:, tp:, :].reshape(MLA_HEADS, n_seq, n_new, HEAD_PAD), (1, 0, 2, 3))
        oh_s = _sample_attention(q_s.reshape(n_seq, MLA_HEADS * n_new, HEAD_PAD), ckv, krp, lw["w_ukt_p"],
                                 lw["w_uv"], cache_kv_latent, cache_kr_t, page_table, l, tp, n_new)

        ys_p, st_p = _ssd_prompt(z, xbc, misc, lw, bsz, seq)
        sp = jnp.pad(state_conv[l], ((0, 0), (0, n_new - hist), (0, 0))).reshape(ts, CONV_DIM)
        h0 = state_ssm[l].reshape(n_seq, D_INNER, D_STATE)
        ys_s, st_s = _ssd_sample(z, xbc, misc, sp, h0, lw, tp, n_seq, n_new)

        (gm_p,) = _gmlp(u, v, lw["w_sp_p"], mask_p, lw["bias_p"], lw["gm_norm"], 0, tp, False)
        gm_s, vn_s = _gmlp(u, v, lw["w_sp_s"], mask_s, lw["bias_s"], lw["gm_norm"], tp, ts, True)

        x, qm = _merge(x, (oh_p, ys_p, gm_p), (oh_s, ys_s, gm_s), lw, tp)

        kv = _memory_kv(mem2d, lw["mem_kv_norm"], lw["w_mem_kv"])
        mk_p = kv[:, :MEM_WIDTH].reshape(bsz, mem_tokens, MEM_WIDTH)
        mv_p = kv[:, MEM_WIDTH:].reshape(bsz, mem_tokens, MEM_WIDTH)
        om_p = _mem_attention(qm, mk_p, mv_p, 0, bsz, seq)
        om_s = _mem_attention(qm, mem_k_t, mem_v_t, tp, n_seq, n_new, layer=l)
        om = (om_p, om_s)

        if l == depth - 1:
            y_p, y_s = _ffn(x, ffn_norm[l, 1], w_fi, w_fo, l, 1, tp, extra=om, w_extra=lw["w_mem_o"],
                            final_g=final_norm)
        else:
            x = _ffn(x, ffn_norm[l, 1], w_fi, w_fo, l, 1, tp, extra=om, w_extra=lw["w_mem_o"])
            x = _ffn(x, ffn_norm[l + 1, 0], w_fi, w_fo, l + 1, 0, tp)

        outs["kvl_p"].append(ckv[:tp].reshape(bsz, seq, KV_LORA))
        outs["kr_p"].append(krp[:tp, ROPE_LO:ROPE_LO + QK_ROPE].reshape(bsz, seq, QK_ROPE))
        outs["ssm_p"].append(st_p.reshape(bsz, SSM_HEADS, SSM_HEAD_DIM, D_STATE))
        outs["conv_p"].append(jnp.stack([xbc[(b + 1) * seq - hist:(b + 1) * seq] for b in range(bsz)]))
        outs["mk_p"].append(mk_p.reshape(bsz, mem_tokens, MEM_HEADS, MEM_HEAD_DIM))
        outs["mv_p"].append(mv_p.reshape(bsz, mem_tokens, MEM_HEADS, MEM_HEAD_DIM))
        outs["kvl_s"].append(ckv[tp:].reshape(n_seq, n_new, KV_LORA))
        outs["kr_s"].append(krp[tp:, ROPE_LO:ROPE_LO + QK_ROPE].reshape(n_seq, n_new, QK_ROPE))
        outs["ssm_s"].append(st_s.reshape(n_seq, SSM_HEADS, SSM_HEAD_DIM, D_STATE))
        outs["conv_s"].append(xbc[tp:].reshape(n_seq, n_new, CONV_DIM)[:, n_new - hist:])
        outs["gv_s"].append(vn_s.reshape(n_seq, n_new, GM_WIDTH))

    st = {k_: jnp.stack(v_) for k_, v_ in outs.items()}
    return (y_p.reshape(bsz, seq, D_MODEL), y_s.reshape(n_seq, n_new, D_MODEL),
            st["kvl_p"], st["kr_p"], st["ssm_p"], st["conv_p"], st["mk_p"], st["mv_p"],
            st["kvl_s"], st["kr_s"], st["ssm_s"], st["conv_s"], st["gv_s"])
```

```python
import functools
import math

import numpy as np
import jax
import jax.numpy as jnp
from jax import lax
from jax.experimental import pallas as pl
from jax.experimental.pallas import tpu as pltpu

F32 = jnp.float32
BF16 = jnp.bfloat16

D_MODEL = 1024
EPS = 1e-6
NEG_INF = -1e30
MLA_HEADS = 8
Q_LORA = 384
KV_LORA = 256
QK_NOPE = 64
QK_ROPE = 32
V_HEAD = 64
ROPE_THETA = 10000.0
MLA_SCALE = 1.0 / math.sqrt(QK_NOPE + QK_ROPE)
LOG2E = math.log2(math.e)
SSM_HEADS = 8
SSM_HEAD_DIM = 64
D_INNER = SSM_HEADS * SSM_HEAD_DIM
SSM_GROUPS = 2
D_STATE = 64
CONV_K = 4
CONV_DIM = D_INNER + 2 * SSM_GROUPS * D_STATE
GM_GROUPS = 8
GM_GROUP_DIM = 64
GM_WIDTH = GM_GROUPS * GM_GROUP_DIM
MEM_HEADS = 4
MEM_HEAD_DIM = 64
MEM_WIDTH = MEM_HEADS * MEM_HEAD_DIM
MEM_SCALE = 1.0 / math.sqrt(MEM_HEAD_DIM)
D_FF = 2816
IN_SIZES = (Q_LORA, KV_LORA, QK_ROPE, D_INNER, CONV_DIM, SSM_HEADS, GM_WIDTH, GM_WIDTH)

LANES = 128
CHUNK = 128
HEAD_PAD = LANES
ROPE_LO = QK_NOPE
ROPE_HALF = QK_ROPE // 2
MISC_DT = 96
VMEM_LIMIT = 56 * 1024 * 1024

P_CQ, P_CKV, P_Z, P_XBC, P_U, P_V, P_MISC, P_END = 0, 384, 640, 1152, 1920, 2432, 2944, 3072

NT_DIMS = (((1,), (1,)), ((), ()))
TN_DIMS = (((0,), (0,)), ((), ()))


def _pick(n, cands):
    for c in cands:
        if n % c == 0:
            return c
    raise ValueError(f"no tile in {cands} divides {n}")


def _rms(x, g):
    return x * lax.rsqrt(jnp.mean(x * x, axis=-1, keepdims=True) + EPS) * g


def _silu(x):
    return x * (1.0 / (1.0 + jnp.exp(-x)))


def _dot(a, b):
    return jnp.dot(a, b, preferred_element_type=F32)


def _dot_nt(a, b):
    return lax.dot_general(a, b, NT_DIMS, preferred_element_type=F32)


def _dot_exact(a, b, dims=None):
    if dims is None:
        return jnp.dot(a, b, precision=lax.Precision.HIGHEST, preferred_element_type=F32)
    return lax.dot_general(a, b, dims, precision=lax.Precision.HIGHEST, preferred_element_type=F32)


def _lane_tile(x, n):
    return x if n == 1 else jnp.concatenate([x] * n, axis=1)


def _const_spec(shape):
    nd = len(shape)
    return pl.BlockSpec(shape, lambda *_: (0,) * nd)


def _resident_spec(shape):
    nd = len(shape)
    return pl.BlockSpec(shape, lambda *_: (0,) * nd, pipeline_mode=pl.Buffered(1))


def _params(sem):
    return pltpu.CompilerParams(dimension_semantics=sem, vmem_limit_bytes=VMEM_LIMIT)


FF_SPLIT = 2


def _ffn_body(*refs, x_grouped, has_extra, has_final, npb):
    it = iter(refs)
    x_refs = (next(it), next(it)) if x_grouped else (next(it),) * 2
    e_refs = (None, None)
    if has_extra:
        e_refs = (next(it), next(it))
        we_ref = next(it)
    g_ref, win_ref, wout_ref = next(it), next(it), next(it)
    if has_final:
        fg_ref = next(it)
    o_refs = (next(it), next(it)) if has_final else (next(it),) * 2

    def compute(x_ref, e_ref, o_ref):
        x = x_ref[...]
        if has_extra:
            x = x + _dot(e_ref[...].astype(BF16), we_ref[...])
        h = _rms(x, g_ref[...]).astype(BF16)
        cw = D_FF // FF_SPLIT
        acc = jnp.zeros_like(x)
        for c in range(FF_SPLIT):
            gate = _dot(h, win_ref[:, c * cw:(c + 1) * cw])
            up = _dot(h, win_ref[:, D_FF + c * cw:D_FF + (c + 1) * cw])
            act = (_silu(gate) * up).astype(BF16)
            acc = acc + _dot(act, wout_ref[c * cw:(c + 1) * cw, :])
        out = x + 0.5 * acc
        o_ref[...] = _rms(out, fg_ref[...]) if has_final else out

    if x_grouped or has_extra or has_final:
        @pl.when(pl.program_id(0) < npb)
        def _():
            compute(x_refs[0], e_refs[0], o_refs[0])

        @pl.when(pl.program_id(0) >= npb)
        def _():
            compute(x_refs[1], e_refs[1], o_refs[1])
    else:
        compute(x_refs[0], None, o_refs[0])


def _ffn(x, norm_g, w_in_all, w_out_all, layer, which, tp, extra=None, w_extra=None, final_g=None):
    x_grouped = isinstance(x, tuple)
    t = x[0].shape[0] + x[1].shape[0] if x_grouped else x.shape[0]
    has_extra = extra is not None
    has_final = final_g is not None
    tm = _pick(math.gcd(tp, t - tp), (512, 256, 128, 64, 32, 16, 8))
    npb = tp // tm
    row = lambda i: (i, 0)
    prow, srow = _group_row_maps(npb)
    if x_grouped:
        args = [x[0], x[1]]
        specs = [pl.BlockSpec((tm, D_MODEL), prow), pl.BlockSpec((tm, D_MODEL), srow)]
    else:
        args, specs = [x], [pl.BlockSpec((tm, D_MODEL), row)]
    if has_extra:
        args += [extra[0], extra[1], w_extra]
        specs += [pl.BlockSpec((tm, extra[0].shape[1]), prow), pl.BlockSpec((tm, extra[1].shape[1]), srow),
                  _resident_spec(w_extra.shape)]
    pick = lambda i: (layer, which, 0, 0)
    args += [norm_g.reshape(1, D_MODEL), w_in_all, w_out_all]
    specs += [_const_spec((1, D_MODEL)),
              pl.BlockSpec((None, None) + w_in_all.shape[2:], pick, pipeline_mode=pl.Buffered(1)),
              pl.BlockSpec((None, None) + w_out_all.shape[2:], pick, pipeline_mode=pl.Buffered(1))]
    if has_final:
        args.append(final_g.reshape(1, D_MODEL))
        specs.append(_const_spec((1, D_MODEL)))
        out_shape = [jax.ShapeDtypeStruct((tp, D_MODEL), F32), jax.ShapeDtypeStruct((t - tp, D_MODEL), F32)]
        out_specs = [pl.BlockSpec((tm, D_MODEL), prow), pl.BlockSpec((tm, D_MODEL), srow)]
    else:
        out_shape = [jax.ShapeDtypeStruct((t, D_MODEL), F32)]
        out_specs = [pl.BlockSpec((tm, D_MODEL), row)]
    res = pl.pallas_call(
        functools.partial(_ffn_body, x_grouped=x_grouped, has_extra=has_extra, has_final=has_final, npb=npb),
        grid=(t // tm,), in_specs=specs, out_specs=out_specs, out_shape=out_shape,
        input_output_aliases={} if (x_grouped or has_final) else {0: 0},
        compiler_params=_params(("arbitrary",) if has_final else ("parallel",)), name="ffn",
    )(*args)
    return res if has_final else res[0]


def _rope(blk, c, s1, s2):
    return blk * c + pltpu.roll(blk, ROPE_HALF, 1) * s1 + pltpu.roll(blk, LANES - ROPE_HALF, 1) * s2


def _mixin_body(x_ref, g_ref, win_ref, qn_ref, wuq_ref, kvn_ref, wuk_ref, c_ref, s1_ref, s2_ref,
                q_ref, k_ref, ckv_ref, ckvb_ref, krp_ref, misc_ref, z_ref, xbc_ref, u_ref, v_ref):
    h = _rms(x_ref[...], g_ref[...]).astype(BF16)
    p = _dot(h, win_ref[...])
    z_ref[...] = p[:, P_Z:P_XBC]
    xbc_ref[...] = p[:, P_XBC:P_U]
    u_ref[...] = p[:, P_U:P_V]
    v_ref[...] = p[:, P_V:P_MISC]
    misc = p[:, P_MISC:P_END]
    misc_ref[...] = misc
    c, s1, s2 = c_ref[...], s1_ref[...], s2_ref[...]
    krp = _rope(misc, c, s1, s2)
    krp_ref[...] = krp
    ckvn = _rms(p[:, P_CKV:P_Z], kvn_ref[...])
    ckv_ref[...] = ckvn
    ckvb = ckvn.astype(BF16)
    ckvb_ref[...] = ckvb
    kn = _dot(ckvb, wuk_ref[...])
    qn = _rms(p[:, P_CQ:P_CKV], qn_ref[...]).astype(BF16)
    q = _dot(qn, wuq_ref[...]) * (MLA_SCALE * LOG2E)
    for hh in range(MLA_HEADS):
        sl = slice(hh * HEAD_PAD, (hh + 1) * HEAD_PAD)
        q_ref[hh] = _rope(q[:, sl], c, s1, s2).astype(BF16)
        k_ref[hh] = (kn[:, sl] + krp).astype(BF16)


def _mixin_tile(tp, ts, seq):
    return _pick(math.gcd(math.gcd(tp, ts), seq), (512, 256, 128, 64, 32, 16))


def _mixin(x, lw, tabs, tp, seq):
    t = x.shape[0]
    tm = _mixin_tile(tp, t - tp, seq)
    npb, nqt = tp // tm, seq // tm
    row = lambda i: (i, 0)
    trow = lambda i: (jnp.where(i < npb, i % nqt, nqt), 0)
    hrow = lambda i: (0, i, 0)
    wide = MLA_HEADS * HEAD_PAD
    outs = [
        ((MLA_HEADS, t, HEAD_PAD), BF16, pl.BlockSpec((MLA_HEADS, tm, HEAD_PAD), hrow)),
        ((MLA_HEADS, t, HEAD_PAD), BF16, pl.BlockSpec((MLA_HEADS, tm, HEAD_PAD), hrow)),
        ((t, KV_LORA), F32, pl.BlockSpec((tm, KV_LORA), row)),
        ((t, KV_LORA), BF16, pl.BlockSpec((tm, KV_LORA), row)),
        ((t, LANES), F32, pl.BlockSpec((tm, LANES), row)),
        ((t, LANES), F32, pl.BlockSpec((tm, LANES), row)),
        ((t, D_INNER), F32, pl.BlockSpec((tm, D_INNER), row)),
        ((t, CONV_DIM), F32, pl.BlockSpec((tm, CONV_DIM), row)),
        ((t, GM_WIDTH), F32, pl.BlockSpec((tm, GM_WIDTH), row)),
        ((t, GM_WIDTH), F32, pl.BlockSpec((tm, GM_WIDTH), row)),
    ]
    return pl.pallas_call(
        _mixin_body, grid=(t // tm,),
        in_specs=[pl.BlockSpec((tm, D_MODEL), row), _const_spec((1, D_MODEL)),
                  _resident_spec((D_MODEL, P_END)), _const_spec((1, Q_LORA)),
                  _resident_spec((Q_LORA, wide)), _const_spec((1, KV_LORA)),
                  _resident_spec((KV_LORA, wide)),
                  pl.BlockSpec((tm, LANES), trow), pl.BlockSpec((tm, LANES), trow),
                  pl.BlockSpec((tm, LANES), trow)],
        out_specs=[o[2] for o in outs],
        out_shape=[jax.ShapeDtypeStruct(o[0], o[1]) for o in outs],
        compiler_params=_params(("parallel",)), name="mixin",
    )(x, lw["mix_norm"], lw["w_in_p"], lw["q_norm"], lw["w_uq_p"], lw["kv_norm"], lw["w_uk_p"], *tabs)


def _pattn_body(qi_ref, kj_ref, q_ref, k_ref, v_ref, wuv_ref, o_ref, m_ref, l_ref, acc_ref, *, tq, kvb):
    n = pl.program_id(1)
    i = qi_ref[n]
    j = kj_ref[n]
    last = (i * tq + tq - 1) // kvb

    @pl.when(j == 0)
    def _():
        m_ref[...] = jnp.full(m_ref.shape, NEG_INF, F32)
        l_ref[...] = jnp.zeros(l_ref.shape, F32)
        acc_ref[...] = jnp.zeros(acc_ref.shape, F32)

    def step(masked):
        v = v_ref[...]
        if masked:
            qpos = i * tq + lax.broadcasted_iota(jnp.int32, (tq, kvb), 0)
            kpos = j * kvb + lax.broadcasted_iota(jnp.int32, (tq, kvb), 1)
            ok = kpos <= qpos
        for hh in range(MLA_HEADS):
            s = _dot_nt(q_ref[hh], k_ref[hh])
            if masked:
                s = jnp.where(ok, s, NEG_INF)
            m_prev = m_ref[hh]
            m_new = jnp.maximum(m_prev, jnp.max(s, axis=1, keepdims=True))
            alpha = jnp.exp2(m_prev - m_new)
            p = jnp.exp2(s - _lane_tile(m_new, kvb // LANES))
            psum = p[:, :LANES]
            for t in range(1, kvb // LANES):
                psum = psum + p[:, t * LANES:(t + 1) * LANES]
            l_ref[hh] = alpha * l_ref[hh] + psum
            acc_ref[hh] = acc_ref[hh] * _lane_tile(alpha, KV_LORA // LANES) + _dot(p.astype(BF16), v)
            m_ref[hh] = m_new

    @pl.when(j == last)
    def _():
        step(True)
        for hh in range(MLA_HEADS):
            o = (acc_ref[hh] / jnp.sum(l_ref[hh], axis=1, keepdims=True)).astype(BF16)
            o_ref[:, hh * V_HEAD:(hh + 1) * V_HEAD] = _dot(o, wuv_ref[hh]).astype(BF16)

    @pl.when(j != last)
    def _():
        step(False)


def _prompt_attention(q, k, vb, w_uv, bsz, seq):
    tq = _pick(seq, (512, 256, 128))
    kvb = _pick(seq, (512, 256, 128))
    nq, nkv = seq // tq, seq // kvb
    pairs = [(i, j) for i in range(nq) for j in range((i * tq + tq - 1) // kvb + 1)]
    qi = jnp.asarray(np.array([p[0] for p in pairs], np.int32))
    kj = jnp.asarray(np.array([p[1] for p in pairs], np.int32))
    grid_spec = pltpu.PrefetchScalarGridSpec(
        num_scalar_prefetch=2, grid=(bsz, len(pairs)),
        in_specs=[
            pl.BlockSpec((MLA_HEADS, tq, HEAD_PAD), lambda b, n, qi, kj: (0, b * nq + qi[n], 0)),
            pl.BlockSpec((MLA_HEADS, kvb, HEAD_PAD), lambda b, n, qi, kj: (0, b * nkv + kj[n], 0)),
            pl.BlockSpec((kvb, KV_LORA), lambda b, n, qi, kj: (b * nkv + kj[n], 0)),
            pl.BlockSpec((MLA_HEADS, KV_LORA, V_HEAD), lambda b, n, qi, kj: (0, 0, 0)),
        ],
        out_specs=pl.BlockSpec((tq, MLA_HEADS * V_HEAD), lambda b, n, qi, kj: (b * nq + qi[n], 0)),
        scratch_shapes=[pltpu.VMEM((MLA_HEADS, tq, LANES), F32), pltpu.VMEM((MLA_HEADS, tq, LANES), F32),
                        pltpu.VMEM((MLA_HEADS, tq, KV_LORA), F32)],
    )
    return pl.pallas_call(
        functools.partial(_pattn_body, tq=tq, kvb=kvb), grid_spec=grid_spec,
        out_shape=jax.ShapeDtypeStruct((bsz * seq, MLA_HEADS * V_HEAD), BF16),
        compiler_params=_params(("parallel", "arbitrary")), name="prompt_attn",
    )(qi, kj, q, k, vb, w_uv)


def _sattn_body(pt_ref, q_ref, ckvn_ref, krn_ref, wukt_ref, wuv_ref, ckv_hbm, ckr_hbm, o_ref,
                kbuf, rbuf, sem, qlat_ref, qrope_ref, m_ref, l_ref, acc_ref, *, layer, pc, n_seq, n_ch, page, nsub):
    b = pl.program_id(0)
    c = pl.program_id(1)
    step = b * n_ch + c
    slot = step % 2
    n_new = q_ref.shape[1] // MLA_HEADS

    def copies(bb, cc, sl):
        kv, kr = [], []
        for pg in range(pc):
            pid = pt_ref[bb, cc * pc + pg]
            kv.append(pltpu.make_async_copy(ckv_hbm.at[layer, pid], kbuf.at[sl, pl.ds(pg * page, page)], sem.at[0, sl]))
            kr.append(pltpu.make_async_copy(ckr_hbm.at[layer, pid], rbuf.at[sl, pg], sem.at[1, sl]))
        return kv + kr

    def start_all(cps):
        for t, cp in enumerate(cps):
            cp.start(priority=t % 2)

    @pl.when(step == 0)
    def _():
        start_all(copies(b, c, slot))

    @pl.when(step + 1 < n_seq * n_ch)
    def _():
        wrap = c + 1 == n_ch
        nb = jnp.where(wrap, b + 1, b)
        nc = jnp.where(wrap, 0, c + 1)
        start_all(copies(nb, nc, 1 - slot))

    @pl.when(c == 0)
    def _():
        m_ref[...] = jnp.full(m_ref.shape, NEG_INF, F32)
        l_ref[...] = jnp.zeros(l_ref.shape, F32)
        acc_ref[...] = jnp.zeros(acc_ref.shape, F32)
        q32 = q_ref[0].astype(F32)
        for hh in range(MLA_HEADS):
            qh = q32[hh * n_new:(hh + 1) * n_new, :].astype(BF16)
            qlat_ref[hh * n_new:(hh + 1) * n_new, :] = _dot(qh, wukt_ref[hh])
        qrope_ref[...] = q32[:, ROPE_LO:ROPE_LO + QK_ROPE]

    def merge(parts):
        m_prev = m_ref[...]
        m_new = m_prev
        for m, _, _ in parts:
            m_new = jnp.maximum(m_new, m)
        alpha = jnp.exp2(m_prev - m_new)
        l_new = alpha * l_ref[...]
        acc = acc_ref[...] * alpha
        for m, l, a in parts:
            w = jnp.exp2(m - m_new)
            l_new = l_new + w * l
            acc = acc + w * a
        m_ref[...] = m_new
        l_ref[...] = l_new
        acc_ref[...] = acc

    for cp in copies(b, c, slot):
        cp.wait()

    def chunk(last):
        qlat = qlat_ref[...].astype(BF16)
        qrope = qrope_ref[...].astype(BF16)
        pps = pc // nsub
        vals, scores = [], []
        for u in range(nsub):
            kc = kbuf[slot, u * pps * page:(u + 1) * pps * page, :].astype(BF16)
            rc = jnp.concatenate([rbuf[slot, u * pps + pg] for pg in range(pps)], axis=1).astype(BF16)
            vals.append(kc)
            scores.append(_dot_nt(qlat, kc) + _dot(qrope, rc))
        if last:
            pad = LANES - n_new
            kn = jnp.concatenate([ckvn_ref[...], jnp.zeros((pad, KV_LORA), F32)], axis=0).astype(BF16)
            rn = jnp.concatenate([krn_ref[...][:, ROPE_LO:ROPE_LO + QK_ROPE], jnp.zeros((pad, QK_ROPE), F32)],
                                 axis=0).astype(BF16)
            s = _dot_nt(qlat, kn) + _dot_nt(qrope, rn)
            tpos = lax.broadcasted_iota(jnp.int32, s.shape, 0) % n_new
            kpos = lax.broadcasted_iota(jnp.int32, s.shape, 1)
            vals.append(kn)
            scores.append(jnp.where(kpos <= tpos, s, NEG_INF))
        maxes = [jnp.max(s, axis=1, keepdims=True) for s in scores]
        probs = [jnp.exp2(s - m) for s, m in zip(scores, maxes)]
        sums = [jnp.sum(p, axis=1, keepdims=True) for p in probs]
        outs = [_dot(p.astype(BF16), v) for p, v in zip(probs, vals)]
        merge(list(zip(maxes, sums, outs)))
        if last:
            o = acc_ref[...] / l_ref[...]
            pieces = [_dot(o[hh * n_new:(hh + 1) * n_new, :].astype(BF16), wuv_ref[hh]) for hh in range(MLA_HEADS)]
            o_ref[0] = jnp.concatenate(pieces, axis=1).astype(BF16)

    @pl.when(c == n_ch - 1)
    def _():
        chunk(True)

    @pl.when(c != n_ch - 1)
    def _():
        chunk(False)


SATTN_PAGES = 64
SATTN_SPLIT = 2


def _sample_attention(q_s, ckv_all, krp_all, w_ukt_p, w_uv, cache_kv, cache_kr_t, page_table, layer, tp, n_new):
    n_seq, n_pages = page_table.shape
    page = cache_kv.shape[2]
    pc = _pick(n_pages, (SATTN_PAGES, 16, 8, 4, 2, 1))
    nsub = SATTN_SPLIT if pc % SATTN_SPLIT == 0 else 1
    n_ch = n_pages // pc
    rows = MLA_HEADS * n_new
    off = tp // n_new
    grid_spec = pltpu.PrefetchScalarGridSpec(
        num_scalar_prefetch=1, grid=(n_seq, n_ch),
        in_specs=[
            pl.BlockSpec((1, rows, HEAD_PAD), lambda b, c, pt: (b, 0, 0)),
            pl.BlockSpec((n_new, KV_LORA), lambda b, c, pt: (off + b, 0)),
            pl.BlockSpec((n_new, LANES), lambda b, c, pt: (off + b, 0)),
            pl.BlockSpec((MLA_HEADS, HEAD_PAD, KV_LORA), lambda b, c, pt: (0, 0, 0)),
            pl.BlockSpec((MLA_HEADS, KV_LORA, V_HEAD), lambda b, c, pt: (0, 0, 0)),
            pl.BlockSpec(memory_space=pl.ANY),
            pl.BlockSpec(memory_space=pl.ANY),
        ],
        out_specs=pl.BlockSpec((1, n_new, MLA_HEADS * V_HEAD), lambda b, c, pt: (b, 0, 0)),
        scratch_shapes=[
            pltpu.VMEM((2, pc * page, KV_LORA), F32), pltpu.VMEM((2, pc, QK_ROPE, page), F32),
            pltpu.SemaphoreType.DMA((2, 2)),
            pltpu.VMEM((rows, KV_LORA), F32), pltpu.VMEM((rows, QK_ROPE), F32),
            pltpu.VMEM((rows, 1), F32), pltpu.VMEM((rows, 1), F32), pltpu.VMEM((rows, KV_LORA), F32),
        ],
    )
    out = pl.pallas_call(
        functools.partial(_sattn_body, layer=layer, pc=pc, n_seq=n_seq, n_ch=n_ch, page=page, nsub=nsub),
        grid_spec=grid_spec,
        out_shape=jax.ShapeDtypeStruct((n_seq, n_new, MLA_HEADS * V_HEAD), BF16),
        compiler_params=_params(("arbitrary", "arbitrary")), name="sample_attn",
    )(page_table, q_s, ckv_all, krp_all, w_ukt_p, w_uv, cache_kv, cache_kr_t)
    return out.reshape(n_seq * n_new, MLA_HEADS * V_HEAD)


def _ssd_consts(seg):
    r = np.arange(CHUNK)
    same = (r[:, None] // seg) == (r[None, :] // seg)
    caus = (same & (r[None, :] <= r[:, None])).astype(np.float32)
    plast = (r[None, :] == (r[:, None] // seg) * seg + seg - 1).astype(np.float32)
    et = np.zeros((D_INNER, LANES), np.float32)
    et[np.arange(D_INNER), MISC_DT + np.arange(D_INNER) // SSM_HEAD_DIM] = 1.0
    return jnp.asarray(caus), jnp.asarray(plast), jnp.asarray(et)


def _ssd_tile(act, z, misc, caus, plast, et, dtb, alog, dskip, ng, h_in_fn, seg):
    nseg = CHUNK // seg
    lane = lax.broadcasted_iota(jnp.int32, (1, LANES), 1)
    hmask = (lane >= MISC_DT) & (lane < MISC_DT + SSM_HEADS)
    lo = lax.broadcasted_iota(jnp.int32, (CHUNK, LANES), 1) < SSM_HEAD_DIM
    xs = act[:, :D_INNER]
    bm = act[:, D_INNER:D_INNER + LANES]
    cm = act[:, D_INNER + LANES:D_INNER + 2 * LANES]
    dtv = jnp.where(hmask, jax.nn.softplus(misc + dtb), 0.0)
    a_row = jnp.where(hmask, -jnp.exp(alog), 0.0)
    acs = _dot_exact(caus, dtv * a_row)
    acs_last = _dot_exact(plast, acs)
    e_acs = jnp.exp(acs)
    to_end = jnp.exp(acs_last - acs)
    e_last = jnp.exp(acs_last)
    acs_t = acs.T
    dec_all = _dot_exact(et, e_last, NT_DIMS)

    bm_b = bm.astype(BF16)
    cm_g = [jnp.where(lo, cm, 0.0).astype(BF16), jnp.where(lo, 0.0, cm).astype(BF16)]
    g_mat = [_dot_nt(cm_g[g], bm_b) for g in range(SSM_GROUPS)]
    cm_half = [cm[:, :D_STATE].astype(BF16), cm[:, D_STATE:].astype(BF16)]

    def col(arr, hh):
        return arr[:, MISC_DT + hh:MISC_DT + hh + 1]

    def pair(arr, pb):
        return jnp.where(lo, col(arr, 2 * pb), col(arr, 2 * pb + 1))

    hpg = SSM_HEADS // SSM_GROUPS
    rows_g = hpg * SSM_HEAD_DIM
    y_off = []
    for g in range(SSM_GROUPS):
        parts = []
        for j in range(nseg):
            hj = h_in_fn(j)[g * rows_g:(g + 1) * rows_g, :].astype(BF16)
            parts.append(_dot_nt(cm_half[g][j * seg:(j + 1) * seg, :], hj))
        y_off.append(parts[0] if nseg == 1 else jnp.concatenate(parts, axis=0))

    y_blocks, xte_blocks = [], []
    for pb in range(SSM_HEADS // 2):
        g = (2 * pb) // hpg
        sl = slice(pb * LANES, (pb + 1) * LANES)
        xs_p = xs[:, sl]
        xdt = xs_p * pair(dtv, pb)
        xte_blocks.append((xdt * pair(to_end, pb)).astype(BF16))
        xdt_b = xdt.astype(BF16)
        yd = []
        for hh in (2 * pb, 2 * pb + 1):
            segm = col(acs, hh) - acs_t[MISC_DT + hh:MISC_DT + hh + 1, :]
            decay = jnp.exp(jnp.where(caus > 0.0, segm, NEG_INF))
            yd.append(_dot((g_mat[g] * decay).astype(BF16), xdt_b))
        y_d = jnp.where(lo, yd[0], yd[1])
        off_sl = slice((pb % (hpg // 2)) * LANES, (pb % (hpg // 2) + 1) * LANES)
        y_o = y_off[g][:, off_sl] * pair(e_acs, pb)
        y_blocks.append(y_d + y_o + dskip[:, sl] * xs_p)
    y = jnp.concatenate(y_blocks, axis=1)
    yn = _rms(y * _silu(z), ng)

    xte = jnp.concatenate(xte_blocks, axis=1)
    rowi = lax.broadcasted_iota(jnp.int32, (D_INNER, D_STATE), 0) < rows_g
    new_states = []
    if nseg == 1:
        s_all = lax.dot_general(xte, bm_b, TN_DIMS, preferred_element_type=F32)
        s_sel = jnp.where(rowi, s_all[:, :D_STATE], s_all[:, D_STATE:])
        new_states.append(h_in_fn(0) * dec_all[:, 0:1] + s_sel)
    else:
        tr = lax.broadcasted_iota(jnp.int32, (CHUNK, LANES), 0) // seg
        bbd = jnp.concatenate([jnp.where(tr == j, bm, 0.0) for j in range(nseg)], axis=1).astype(BF16)
        s_all = lax.dot_general(xte, bbd, TN_DIMS, preferred_element_type=F32)
        for j in range(nseg):
            sj = s_all[:, j * LANES:(j + 1) * LANES]
            s_sel = jnp.where(rowi, sj[:, :D_STATE], sj[:, D_STATE:])
            new_states.append(h_in_fn(j) * dec_all[:, j * seg:j * seg + 1] + s_sel)
    return yn, new_states


def _ssd_prompt_body(*refs, bsz):
    zs, xbcs, miscs = refs[:bsz], refs[bsz:2 * bsz], refs[2 * bsz:3 * bsz]
    (caus_ref, plast_ref, et_ref, cw_ref, cb_ref, dtb_ref, alog_ref, dskip_ref, ng_ref,
     y_ref, st_ref, xpad_ref) = refs[3 * bsz:]
    c = pl.program_id(0)
    hist = CONV_K - 1

    @pl.when(c == 0)
    def _():
        xpad_ref[:, 0:8, :] = jnp.zeros((bsz, 8, CONV_DIM), F32)
        st_ref[...] = jnp.zeros(st_ref.shape, F32)

    for b in range(bsz):
        x = xbcs[b][...]
        xpad_ref[b, 8:8 + CHUNK, :] = x
        conv = cb_ref[...] + x * cw_ref[hist:hist + 1, :]
        for k in range(hist):
            conv = conv + xpad_ref[b, 8 - hist + k:8 - hist + k + CHUNK, :] * cw_ref[k:k + 1, :]
        xpad_ref[b, 0:8, :] = x[CHUNK - 8:, :]
        act = _silu(conv)
        yn, new = _ssd_tile(act, zs[b][...], miscs[b][...], caus_ref[...], plast_ref[...], et_ref[...],
                            dtb_ref[...], alog_ref[...], dskip_ref[...], ng_ref[...], lambda j, b=b: st_ref[b], CHUNK)
        y_ref[b] = yn.astype(BF16)
        st_ref[b] = new[0]


def _ssd_sample_body(z_ref, xbc_ref, misc_ref, sp_ref, h0_ref, caus_ref, plast_ref, et_ref, cw_ref, cb_ref,
                     dtb_ref, alog_ref, dskip_ref, ng_ref, y_ref, st_ref, *, seg):
    hist = CONV_K - 1
    x = xbc_ref[...]
    sp = sp_ref[...]
    tpos = lax.broadcasted_iota(jnp.int32, (CHUNK, CONV_DIM), 0) % seg
    conv = cb_ref[...] + x * cw_ref[hist:hist + 1, :]
    for j in range(1, CONV_K):
        xin = pltpu.roll(x, j, 0)
        up = hist - j
        sin = sp if up == 0 else pltpu.roll(sp, CHUNK - up, 0)
        conv = conv + jnp.where(tpos >= j, xin, sin) * cw_ref[hist - j:hist - j + 1, :]
    act = _silu(conv)
    yn, new = _ssd_tile(act, z_ref[...], misc_ref[...], caus_ref[...], plast_ref[...], et_ref[...],
                        dtb_ref[...], alog_ref[...], dskip_ref[...], ng_ref[...], lambda j: h0_ref[j], seg)
    y_ref[...] = yn.astype(BF16)
    for j in range(len(new)):
        st_ref[j] = new[j]


def _ssd_param_args(lw):
    return [lw["conv_w"], lw["conv_b"], lw["dtb_row"], lw["alog_row"], lw["dskip_exp"], lw["ssm_norm"]]


def _ssd_param_specs():
    return [_const_spec((CONV_K, CONV_DIM)), _const_spec((1, CONV_DIM)), _const_spec((1, LANES)),
            _const_spec((1, LANES)), _const_spec((1, D_INNER)), _const_spec((1, D_INNER))]


def _ssd_prompt(z, xbc, misc, lw, bsz, seq):
    nc = seq // CHUNK
    caus, plast, et = _ssd_consts(CHUNK)
    rows = [(lambda c, b=b: (b * nc + c, 0)) for b in range(bsz)]
    y, st = pl.pallas_call(
        functools.partial(_ssd_prompt_body, bsz=bsz), grid=(nc,),
        in_specs=[pl.BlockSpec((CHUNK, D_INNER), r) for r in rows]
        + [pl.BlockSpec((CHUNK, CONV_DIM), r) for r in rows] + [pl.BlockSpec((CHUNK, LANES), r) for r in rows]
        + [_const_spec((CHUNK, CHUNK)), _const_spec((CHUNK, CHUNK)), _const_spec((D_INNER, LANES))]
        + _ssd_param_specs(),
        out_specs=[pl.BlockSpec((bsz, CHUNK, D_INNER), lambda c: (0, c, 0)),
                   pl.BlockSpec((bsz, D_INNER, D_STATE), lambda c: (0, 0, 0))],
        out_shape=[jax.ShapeDtypeStruct((bsz, seq, D_INNER), BF16),
                   jax.ShapeDtypeStruct((bsz, D_INNER, D_STATE), F32)],
        scratch_shapes=[pltpu.VMEM((bsz, 8 + CHUNK, CONV_DIM), F32)],
        compiler_params=_params(("arbitrary",)), name="ssd_prompt",
    )(*([z] * bsz), *([xbc] * bsz), *([misc] * bsz), caus, plast, et, *_ssd_param_args(lw))
    return y.reshape(bsz * seq, D_INNER), st


def _ssd_sample(z, xbc, misc, sp, h0, lw, tp, n_seq, seg):
    nseg = CHUNK // seg
    nt = n_seq // nseg
    off = tp // CHUNK
    caus, plast, et = _ssd_consts(seg)
    row = lambda i: (off + i, 0)
    y, st = pl.pallas_call(
        functools.partial(_ssd_sample_body, seg=seg), grid=(nt,),
        in_specs=[pl.BlockSpec((CHUNK, D_INNER), row), pl.BlockSpec((CHUNK, CONV_DIM), row),
                  pl.BlockSpec((CHUNK, LANES), row), pl.BlockSpec((CHUNK, CONV_DIM), lambda i: (i, 0)),
                  pl.BlockSpec((nseg, D_INNER, D_STATE), lambda i: (i, 0, 0)),
                  _const_spec((CHUNK, CHUNK)), _const_spec((CHUNK, CHUNK)),
                  _const_spec((D_INNER, LANES))] + _ssd_param_specs(),
        out_specs=[pl.BlockSpec((CHUNK, D_INNER), lambda i: (i, 0)),
                   pl.BlockSpec((nseg, D_INNER, D_STATE), lambda i: (i, 0, 0))],
        out_shape=[jax.ShapeDtypeStruct((n_seq * seg, D_INNER), BF16),
                   jax.ShapeDtypeStruct((n_seq, D_INNER, D_STATE), F32)],
        compiler_params=_params(("parallel",)), name="ssd_sample",
    )(z, xbc, misc, sp, h0, caus, plast, et, *_ssd_param_args(lw))
    return y, st


def _gmlp_body(u_ref, v_ref, w_ref, mask_ref, b_ref, ng_ref, o_ref, *vn_out, reps):
    lo = lax.broadcasted_iota(jnp.int32, (CHUNK, LANES), 1) < GM_GROUP_DIM
    mask = mask_ref[...] > 0.0
    wm = [jnp.where(mask, w_ref[g], 0.0).astype(BF16) for g in range(GM_GROUPS)]
    bias = b_ref[...]
    for r in range(reps):
        rs = slice(r * CHUNK, (r + 1) * CHUNK)
        ug = jax.nn.gelu(u_ref[rs, :])
        vn = _rms(jax.nn.gelu(v_ref[rs, :]), ng_ref[...])
        if vn_out:
            vn_out[0][rs, :] = vn
        vb = vn.astype(BF16)
        for pb in range(GM_GROUPS // 2):
            sl = slice(pb * LANES, (pb + 1) * LANES)
            s = jnp.where(lo, _dot(wm[2 * pb], vb[:, sl]), _dot(wm[2 * pb + 1], vb[:, sl])) + bias[:, sl]
            o_ref[rs, sl] = (ug[:, sl] * s).astype(BF16)


def _gmlp(u, v, w_sp, mask, bias_exp, norm_g, row_off, n_rows, want_vn):
    reps = _pick(n_rows // CHUNK, (4, 2, 1))
    tm = reps * CHUNK
    off = row_off // tm
    row = lambda i: (off + i, 0)
    out_shape = [jax.ShapeDtypeStruct((n_rows, GM_WIDTH), BF16)]
    out_specs = [pl.BlockSpec((tm, GM_WIDTH), lambda i: (i, 0))]
    if want_vn:
        out_shape.append(jax.ShapeDtypeStruct((n_rows, GM_WIDTH), F32))
        out_specs.append(pl.BlockSpec((tm, GM_WIDTH), lambda i: (i, 0)))
    return pl.pallas_call(
        functools.partial(_gmlp_body, reps=reps), grid=(n_rows // tm,),
        in_specs=[pl.BlockSpec((tm, GM_WIDTH), row), pl.BlockSpec((tm, GM_WIDTH), row),
                  _const_spec((GM_GROUPS, CHUNK, CHUNK)), _const_spec((CHUNK, CHUNK)),
                  _const_spec((CHUNK, GM_WIDTH)), _const_spec((1, GM_WIDTH))],
        out_specs=out_specs, out_shape=out_shape,
        compiler_params=_params(("parallel",)), name="gmlp",
    )(u, v, w_sp, mask, bias_exp, norm_g)


def _merge_body(x_ref, ohp_ref, ysp_ref, gmp_ref, ohs_ref, yss_ref, gms_ref, ng_ref, wg_ref, bg_ref, womla_ref,
                wossm_ref, wogm_ref, wout_ref, mqn_ref, wmq_ref, xo_ref, qm_ref, *, npb):
    def compute(oh_ref, ys_ref, gm_ref):
        x = x_ref[...]
        h = _rms(x, ng_ref[...]).astype(BF16)
        gates = _dot(h, wg_ref[...]) + bg_ref[...]
        gates = 1.0 / (1.0 + jnp.exp(-gates))
        mix = (gates[:, :D_MODEL] * _dot(oh_ref[...], womla_ref[...])
               + gates[:, D_MODEL:2 * D_MODEL] * _dot(ys_ref[...], wossm_ref[...])
               + gates[:, 2 * D_MODEL:] * _dot(gm_ref[...], wogm_ref[...]))
        x2 = x + _dot(mix.astype(BF16), wout_ref[...])
        xo_ref[...] = x2
        hq = _rms(x2, mqn_ref[...]).astype(BF16)
        qm_ref[...] = _dot(hq, wmq_ref[...])

    @pl.when(pl.program_id(0) < npb)
    def _():
        compute(ohp_ref, ysp_ref, gmp_ref)

    @pl.when(pl.program_id(0) >= npb)
    def _():
        compute(ohs_ref, yss_ref, gms_ref)


def _group_row_maps(npb):
    return (lambda i: (jnp.minimum(i, npb - 1), 0)), (lambda i: (jnp.maximum(i - npb, 0), 0))


def _merge(x, branches_p, branches_s, lw, tp):
    t = x.shape[0]
    tm = _pick(math.gcd(tp, t - tp), (512, 256, 128, 64, 32, 16))
    row = lambda i: (i, 0)
    prow, srow = _group_row_maps(tp // tm)
    half = MLA_HEADS * V_HEAD
    widths = (half, D_INNER, GM_WIDTH)
    return pl.pallas_call(
        functools.partial(_merge_body, npb=tp // tm), grid=(t // tm,),
        in_specs=[pl.BlockSpec((tm, D_MODEL), row)]
        + [pl.BlockSpec((tm, w), prow) for w in widths] + [pl.BlockSpec((tm, w), srow) for w in widths]
        + [_const_spec((1, D_MODEL)), _resident_spec((D_MODEL, 3 * D_MODEL)), _const_spec((1, 3 * D_MODEL)),
           _resident_spec((half, D_MODEL)), _resident_spec((D_INNER, D_MODEL)),
           _resident_spec((GM_WIDTH, D_MODEL)), _resident_spec((D_MODEL, D_MODEL)),
           _const_spec((1, D_MODEL)), _resident_spec((D_MODEL, MEM_WIDTH))],
        out_specs=[pl.BlockSpec((tm, D_MODEL), row), pl.BlockSpec((tm, MEM_WIDTH), row)],
        out_shape=[jax.ShapeDtypeStruct((t, D_MODEL), F32), jax.ShapeDtypeStruct((t, MEM_WIDTH), F32)],
        input_output_aliases={0: 0}, compiler_params=_params(("parallel",)), name="merge",
    )(x, *branches_p, *branches_s, lw["mix_norm"], lw["w_gate"], lw["b_gate"], lw["w_o_mla"], lw["w_o_ssm"],
      lw["w_o_gm"], lw["w_out"], lw["mem_q_norm"], lw["w_mem_q"])


def _memattn_body(q_ref, k_ref, v_ref, o_ref, *, sb, feat_major):
    q = q_ref[...]
    rows = q.shape[0] // sb
    if feat_major:
        m = k_ref.shape[2]
        kb = jnp.concatenate([k_ref[e] for e in range(sb)], axis=1).astype(BF16)
        vb = jnp.concatenate([v_ref[e] for e in range(sb)], axis=1).astype(BF16)
    else:
        m = k_ref.shape[1]
        kb = k_ref[...].reshape(sb * m, MEM_WIDTH).astype(BF16)
        vb = v_ref[...].reshape(sb * m, MEM_WIDTH).astype(BF16)
    lane = lax.broadcasted_iota(jnp.int32, q.shape, 1) // MEM_HEAD_DIM
    if sb > 1:
        own = (lax.broadcasted_iota(jnp.int32, (q.shape[0], sb * m), 0) // rows
               == lax.broadcasted_iota(jnp.int32, (q.shape[0], sb * m), 1) // m)
    out = jnp.zeros(q.shape, F32)
    for hh in range(MEM_HEADS):
        sel = lane == hh
        qh = jnp.where(sel, q, 0.0).astype(BF16)
        s = (_dot(qh, kb) if feat_major else _dot_nt(qh, kb)) * MEM_SCALE
        if sb > 1:
            s = jnp.where(own, s, NEG_INF)
        p = jnp.exp(s - jnp.max(s, axis=1, keepdims=True))
        p = (p / jnp.sum(p, axis=1, keepdims=True)).astype(BF16)
        out = out + jnp.where(sel, _dot_nt(p, vb) if feat_major else _dot(p, vb), 0.0)
    o_ref[...] = out


MEMATTN_BATCH = 8


def _mem_attention(qm, mem_k, mem_v, row_off, nb, rows_per_b, layer=None):
    tm = _pick(rows_per_b, (512, 256, 128, 64, 32, 16, 8))
    nt = rows_per_b // tm
    sb = MEMATTN_BATCH if (nt == 1 and nb % MEMATTN_BATCH == 0 and tm * MEMATTN_BATCH <= 512) else 1
    off = row_off // (tm * sb)
    if layer is None:
        kv_spec = pl.BlockSpec((sb,) + mem_k.shape[1:], lambda b, i: (b, 0, 0))
    else:
        kv_spec = pl.BlockSpec((None, sb) + mem_k.shape[2:], lambda b, i: (layer, b, 0, 0))
    return pl.pallas_call(
        functools.partial(_memattn_body, sb=sb, feat_major=layer is not None), grid=(nb // sb, nt),
        in_specs=[pl.BlockSpec((tm * sb, MEM_WIDTH), lambda b, i: (off + b * nt + i, 0)), kv_spec, kv_spec],
        out_specs=pl.BlockSpec((tm * sb, MEM_WIDTH), lambda b, i: (b * nt + i, 0)),
        out_shape=jax.ShapeDtypeStruct((nb * rows_per_b, MEM_WIDTH), F32),
        compiler_params=_params(("parallel", "parallel")), name="mem_attn",
    )(qm, mem_k, mem_v)


def _memkv_body(m_ref, g_ref, w_ref, o_ref):
    o_ref[...] = _dot(_rms(m_ref[...], g_ref[...]).astype(BF16), w_ref[...])


def _memory_kv(mem2d, norm_g, w_kv):
    t = mem2d.shape[0]
    tm = _pick(t, (256, 128, 64, 32, 16, 8))
    return pl.pallas_call(
        _memkv_body, grid=(t // tm,),
        in_specs=[pl.BlockSpec((tm, D_MODEL), lambda i: (i, 0)), _const_spec((1, D_MODEL)),
                  _const_spec((D_MODEL, 2 * MEM_WIDTH))],
        out_specs=pl.BlockSpec((tm, 2 * MEM_WIDTH), lambda i: (i, 0)),
        out_shape=jax.ShapeDtypeStruct((t, 2 * MEM_WIDTH), F32),
        compiler_params=_params(("parallel",)), name="mem_kv",
    )(mem2d, norm_g, w_kv)


def _layer_weights(l, p, seg):
    w_in = p["w_in"][l]
    offs = np.cumsum((0,) + IN_SIZES)
    cq, ckv, kr, z, xbc, dt, u, v = [w_in[:, offs[i]:offs[i + 1]] for i in range(len(IN_SIZES))]
    zc = lambda n: jnp.zeros((D_MODEL, n), F32)
    w_in_p = jnp.concatenate(
        [cq, ckv, z, xbc, u, v, zc(ROPE_LO), kr, dt, zc(LANES - ROPE_LO - QK_ROPE - SSM_HEADS)], axis=1)
    hd = QK_NOPE + QK_ROPE
    w_uq_p = jnp.pad(p["w_uq"][l].reshape(Q_LORA, MLA_HEADS, hd), ((0, 0), (0, 0), (0, HEAD_PAD - hd)))
    w_uk = p["w_uk"][l]
    w_uk_p = jnp.pad(jnp.transpose(w_uk, (1, 0, 2)), ((0, 0), (0, 0), (0, HEAD_PAD - QK_NOPE)))
    w_ukt_p = jnp.pad(jnp.transpose(w_uk, (0, 2, 1)), ((0, 0), (0, HEAD_PAD - QK_NOPE), (0, 0)))
    lane_pad = lambda a: jnp.pad(a.reshape(1, SSM_HEADS), ((0, 0), (MISC_DT, LANES - MISC_DT - SSM_HEADS)))
    w_sp = p["w_spatial"][l]
    b_sp = p["b_spatial"][l]
    bias_p = jnp.repeat(b_sp.T, GM_GROUP_DIM, axis=1)
    reps = CHUNK // seg
    w_sp_s = jnp.tile(w_sp[:, :seg, :seg], (1, reps, reps))
    bias_s = jnp.tile(jnp.repeat(b_sp[:, :seg].T, GM_GROUP_DIM, axis=1), (reps, 1))
    row = lambda a: a.reshape(1, -1)
    return {
        "ffn_norm": p["ffn_norm"][l],
        "mix_norm": row(p["mix_norm"][l]), "w_in_p": w_in_p.astype(BF16),
        "q_norm": row(p["q_norm"][l]), "w_uq_p": w_uq_p.reshape(Q_LORA, MLA_HEADS * HEAD_PAD).astype(BF16),
        "kv_norm": row(p["kv_norm"][l]), "w_uk_p": w_uk_p.reshape(KV_LORA, MLA_HEADS * HEAD_PAD).astype(BF16),
        "w_ukt_p": w_ukt_p.astype(BF16), "w_uv": p["w_uv"][l].astype(BF16),
        "w_o_mla": p["w_o_mla"][l].astype(BF16),
        "conv_w": p["conv_w"][l], "conv_b": row(p["conv_b"][l]),
        "dtb_row": lane_pad(p["dt_bias"][l]), "alog_row": lane_pad(p["a_log"][l]),
        "dskip_exp": row(jnp.repeat(p["d_skip"][l], SSM_HEAD_DIM)), "ssm_norm": row(p["ssm_norm"][l]),
        "w_o_ssm": p["w_o_ssm"][l].astype(BF16),
        "gm_norm": row(p["gm_norm"][l]), "w_sp_p": w_sp, "bias_p": bias_p, "w_sp_s": w_sp_s, "bias_s": bias_s,
        "w_o_gm": p["w_o_gm"][l].astype(BF16),
        "w_gate": p["w_gate"][l].astype(BF16), "b_gate": row(p["b_gate"][l]), "w_out": p["w_out"][l].astype(BF16),
        "mem_q_norm": row(p["mem_q_norm"][l]), "mem_kv_norm": row(p["mem_kv_norm"][l]),
        "w_mem_q": p["w_mem_q"][l].astype(BF16), "w_mem_kv": p["w_mem_kv"][l].astype(BF16),
        "w_mem_o": p["w_mem_o"][l].astype(BF16),
    }


def _rope_tables(pos):
    inv = ROPE_THETA ** (-jnp.arange(ROPE_HALF, dtype=F32) / ROPE_HALF)
    ang = pos.astype(F32)[:, None] * inv[None, :]
    cos, sin = jnp.cos(ang), jnp.sin(ang)
    n = pos.shape[0]
    zero = lambda w: jnp.zeros((n, w), F32)
    tail = LANES - ROPE_LO - QK_ROPE
    c = jnp.concatenate([jnp.ones((n, ROPE_LO), F32), cos, cos, zero(tail)], axis=1)
    s1 = jnp.concatenate([zero(ROPE_LO + ROPE_HALF), sin, zero(tail)], axis=1)
    s2 = jnp.concatenate([zero(ROPE_LO), -sin, zero(ROPE_HALF + tail)], axis=1)
    return c, s1, s2


def _spatial_masks(seg):
    r = np.arange(CHUNK)
    same = (r[:, None] // seg) == (r[None, :] // seg)
    return jnp.asarray((same & (r[None, :] <= r[:, None])).astype(np.float32))


def kernel(x_prompt, x_sample, mem_prompt, cache_kv_latent, cache_k_rope, state_ssm, state_conv, cache_mem_k, cache_mem_v, page_table, ffn_norm, w_ffn_in, w_ffn_out, mix_norm, w_in, q_norm, w_uq, kv_norm, w_uk, w_uv, w_o_mla, conv_w, conv_b, dt_bias, a_log, d_skip, ssm_norm, w_o_ssm, gm_norm, w_spatial, b_spatial, w_o_gm, w_gate, b_gate, w_out, mem_q_norm, mem_kv_norm, w_mem_q, w_mem_kv, w_mem_o, final_norm):
    params = dict(ffn_norm=ffn_norm, w_ffn_in=w_ffn_in, w_ffn_out=w_ffn_out, mix_norm=mix_norm, w_in=w_in,
                  q_norm=q_norm, w_uq=w_uq, kv_norm=kv_norm, w_uk=w_uk, w_uv=w_uv, w_o_mla=w_o_mla,
                  conv_w=conv_w, conv_b=conv_b, dt_bias=dt_bias, a_log=a_log, d_skip=d_skip, ssm_norm=ssm_norm,
                  w_o_ssm=w_o_ssm, gm_norm=gm_norm, w_spatial=w_spatial, b_spatial=b_spatial, w_o_gm=w_o_gm,
                  w_gate=w_gate, b_gate=b_gate, w_out=w_out, mem_q_norm=mem_q_norm, mem_kv_norm=mem_kv_norm,
                  w_mem_q=w_mem_q, w_mem_kv=w_mem_kv, w_mem_o=w_mem_o)
    bsz, seq, _ = x_prompt.shape
    n_seq, n_new, _ = x_sample.shape
    depth = w_in.shape[0]
    past = page_table.shape[1] * cache_kv_latent.shape[2]
    tp, ts = bsz * seq, n_seq * n_new
    mem_tokens = mem_prompt.shape[1]
    assert seq % CHUNK == 0 and CHUNK % n_new == 0 and ts % CHUNK == 0 and n_new % 8 == 0

    tab_tile = _mixin_tile(tp, ts, seq)
    pos = jnp.concatenate([jnp.arange(seq, dtype=jnp.int32),
                           jnp.tile(past + jnp.arange(n_new, dtype=jnp.int32), tab_tile // n_new)])
    tabs = _rope_tables(pos)
    mask_p, mask_s = _spatial_masks(CHUNK), _spatial_masks(n_new)
    hist = CONV_K - 1

    mem2d = mem_prompt.reshape(bsz * mem_tokens, D_MODEL)
    cache_kr_t = jnp.swapaxes(cache_k_rope, 2, 3)
    mem_k_t = jnp.transpose(cache_mem_k, (0, 1, 3, 4, 2)).reshape(depth, n_seq, MEM_WIDTH, -1)
    mem_v_t = jnp.transpose(cache_mem_v, (0, 1, 3, 4, 2)).reshape(depth, n_seq, MEM_WIDTH, -1)
    w_fi, w_fo = w_ffn_in.astype(BF16), w_ffn_out.astype(BF16)
    outs = {k: [] for k in ("kvl_p", "kr_p", "ssm_p", "conv_p", "mk_p", "mv_p", "kvl_s", "kr_s", "ssm_s", "conv_s", "gv_s")}
    lws = [_layer_weights(l, params, n_new) for l in range(depth)]
    x = _ffn((x_prompt.reshape(tp, D_MODEL), x_sample.reshape(ts, D_MODEL)), ffn_norm[0, 0], w_fi, w_fo, 0, 0, tp)
    y_p = y_s = None
    for l in range(depth):
        lw = lws[l]
        q, k, ckv, ckvb, krp, misc, z, xbc, u, v = _mixin(x, lw, tabs, tp, seq)

        oh_p = _prompt_attention(q, k, ckvb, lw["w_uv"], bsz, seq)
        q_s = jnp.transpose(q[:, tp:, :].reshape(MLA_HEADS, n_seq, n_new, HEAD_PAD), (1, 0, 2, 3))
        oh_s = _sample_attention(q_s.reshape(n_seq, MLA_HEADS * n_new, HEAD_PAD), ckv, krp, lw["w_ukt_p"],
                                 lw["w_uv"], cache_kv_latent, cache_kr_t, page_table, l, tp, n_new)

        ys_p, st_p = _ssd_prompt(z, xbc, misc, lw, bsz, seq)
        sp = jnp.pad(state_conv[l], ((0, 0), (0, n_new - hist), (0, 0))).reshape(ts, CONV_DIM)
        h0 = state_ssm[l].reshape(n_seq, D_INNER, D_STATE)
        ys_s, st_s = _ssd_sample(z, xbc, misc, sp, h0, lw, tp, n_seq, n_new)

        (gm_p,) = _gmlp(u, v, lw["w_sp_p"], mask_p, lw["bias_p"], lw["gm_norm"], 0, tp, False)
        gm_s, vn_s = _gmlp(u, v, lw["w_sp_s"], mask_s, lw["bias_s"], lw["gm_norm"], tp, ts, True)

        x, qm = _merge(x, (oh_p, ys_p, gm_p), (oh_s, ys_s, gm_s), lw, tp)

        kv = _memory_kv(mem2d, lw["mem_kv_norm"], lw["w_mem_kv"])
        mk_p = kv[:, :MEM_WIDTH].reshape(bsz, mem_tokens, MEM_WIDTH)
        mv_p = kv[:, MEM_WIDTH:].reshape(bsz, mem_tokens, MEM_WIDTH)
        om_p = _mem_attention(qm, mk_p, mv_p, 0, bsz, seq)
        om_s = _mem_attention(qm, mem_k_t, mem_v_t, tp, n_seq, n_new, layer=l)
        om = (om_p, om_s)

        if l == depth - 1:
            y_p, y_s = _ffn(x, ffn_norm[l, 1], w_fi, w_fo, l, 1, tp, extra=om, w_extra=lw["w_mem_o"],
                            final_g=final_norm)
        else:
            x = _ffn(x, ffn_norm[l, 1], w_fi, w_fo, l, 1, tp, extra=om, w_extra=lw["w_mem_o"])
            x = _ffn(x, ffn_norm[l + 1, 0], w_fi, w_fo, l + 1, 0, tp)

        outs["kvl_p"].append(ckv[:tp].reshape(bsz, seq, KV_LORA))
        outs["kr_p"].append(krp[:tp, ROPE_LO:ROPE_LO + QK_ROPE].reshape(bsz, seq, QK_ROPE))
        outs["ssm_p"].append(st_p.reshape(bsz, SSM_HEADS, SSM_HEAD_DIM, D_STATE))
        outs["conv_p"].append(jnp.stack([xbc[(b + 1) * seq - hist:(b + 1) * seq] for b in range(bsz)]))
        outs["mk_p"].append(mk_p.reshape(bsz, mem_tokens, MEM_HEADS, MEM_HEAD_DIM))
        outs["mv_p"].append(mv_p.reshape(bsz, mem_tokens, MEM_HEADS, MEM_HEAD_DIM))
        outs["kvl_s"].append(ckv[tp:].reshape(n_seq, n_new, KV_LORA))
        outs["kr_s"].append(krp[tp:, ROPE_LO:ROPE_LO + QK_ROPE].reshape(n_seq, n_new, QK_ROPE))
        outs["ssm_s"].append(st_s.reshape(n_seq, SSM_HEADS, SSM_HEAD_DIM, D_STATE))
        outs["conv_s"].append(xbc[tp:].reshape(n_seq, n_new, CONV_DIM)[:, n_new - hist:])
        outs["gv_s"].append(vn_s.reshape(n_seq, n_new, GM_WIDTH))

    st = {k_: jnp.stack(v_) for k_, v_ in outs.items()}
    return (y_p.reshape(bsz, seq, D_MODEL), y_s.reshape(n_seq, n_new, D_MODEL),
            st["kvl_p"], st["kr_p"], st["ssm_p"], st["conv_p"], st["mk_p"], st["mv_p"],
            st["kvl_s"], st["kr_s"], st["ssm_s"], st["conv_s"], st["gv_s"])
```

```python
import functools
import math

import numpy as np
import jax
import jax.numpy as jnp
from jax import lax
from jax.experimental import pallas as pl
from jax.experimental.pallas import tpu as pltpu

F32 = jnp.float32
BF16 = jnp.bfloat16

D_MODEL = 1024
EPS = 1e-6
NEG_INF = -1e30
MLA_HEADS = 8
Q_LORA = 384
KV_LORA = 256
QK_NOPE = 64
QK_ROPE = 32
V_HEAD = 64
ROPE_THETA = 10000.0
MLA_SCALE = 1.0 / math.sqrt(QK_NOPE + QK_ROPE)
LOG2E = math.log2(math.e)
SSM_HEADS = 8
SSM_HEAD_DIM = 64
D_INNER = SSM_HEADS * SSM_HEAD_DIM
SSM_GROUPS = 2
D_STATE = 64
CONV_K = 4
CONV_DIM = D_INNER + 2 * SSM_GROUPS * D_STATE
GM_GROUPS = 8
GM_GROUP_DIM = 64
GM_WIDTH = GM_GROUPS * GM_GROUP_DIM
MEM_HEADS = 4
MEM_HEAD_DIM = 64
MEM_WIDTH = MEM_HEADS * MEM_HEAD_DIM
MEM_SCALE = 1.0 / math.sqrt(MEM_HEAD_DIM)
D_FF = 2816
IN_SIZES = (Q_LORA, KV_LORA, QK_ROPE, D_INNER, CONV_DIM, SSM_HEADS, GM_WIDTH, GM_WIDTH)

LANES = 128
CHUNK = 128
HEAD_PAD = LANES
ROPE_LO = QK_NOPE
ROPE_HALF = QK_ROPE // 2
MISC_DT = 96
VMEM_LIMIT = 56 * 1024 * 1024

P_CQ, P_CKV, P_Z, P_XBC, P_U, P_V, P_MISC, P_END = 0, 384, 640, 1152, 1920, 2432, 2944, 3072

NT_DIMS = (((1,), (1,)), ((), ()))
TN_DIMS = (((0,), (0,)), ((), ()))


def _pick(n, cands):
    for c in cands:
        if n % c == 0:
            return c
    raise ValueError(f"no tile in {cands} divides {n}")


def _rms(x, g):
    return x * lax.rsqrt(jnp.mean(x * x, axis=-1, keepdims=True) + EPS) * g


def _silu(x):
    return x * (1.0 / (1.0 + jnp.exp(-x)))


def _dot(a, b):
    return jnp.dot(a, b, preferred_element_type=F32)


def _dot_nt(a, b):
    return lax.dot_general(a, b, NT_DIMS, preferred_element_type=F32)


def _dot_exact(a, b, dims=None):
    if dims is None:
        return jnp.dot(a, b, precision=lax.Precision.HIGHEST, preferred_element_type=F32)
    return lax.dot_general(a, b, dims, precision=lax.Precision.HIGHEST, preferred_element_type=F32)


def _lane_tile(x, n):
    return x if n == 1 else jnp.concatenate([x] * n, axis=1)


def _const_spec(shape):
    nd = len(shape)
    return pl.BlockSpec(shape, lambda *_: (0,) * nd)


def _resident_spec(shape):
    nd = len(shape)
    return pl.BlockSpec(shape, lambda *_: (0,) * nd, pipeline_mode=pl.Buffered(1))


def _params(sem):
    return pltpu.CompilerParams(dimension_semantics=sem, vmem_limit_bytes=VMEM_LIMIT)


FF_SPLIT = 11


def _ffn_body(*refs, x_grouped, has_extra, has_final, npb):
    it = iter(refs)
    x_refs = (next(it), next(it)) if x_grouped else (next(it),) * 2
    e_refs = (None, None)
    if has_extra:
        e_refs = (next(it), next(it))
        we_ref = next(it)
    g_ref, win_ref, wout_ref = next(it), next(it), next(it)
    if has_final:
        fg_ref = next(it)
    o_refs = (next(it), next(it)) if has_final else (next(it),) * 2

    def compute(x_ref, e_ref, o_ref):
        x = x_ref[...]
        if has_extra:
            x = x + _dot(e_ref[...].astype(BF16), we_ref[...])
        h = _rms(x, g_ref[...]).astype(BF16)
        cw = D_FF // FF_SPLIT
        acc = jnp.zeros_like(x)
        for c in range(FF_SPLIT):
            gate = _dot(h, win_ref[:, c * cw:(c + 1) * cw])
            up = _dot(h, win_ref[:, D_FF + c * cw:D_FF + (c + 1) * cw])
            act = (_silu(gate) * up).astype(BF16)
            acc = acc + _dot(act, wout_ref[c * cw:(c + 1) * cw, :])
        out = x + 0.5 * acc
        o_ref[...] = _rms(out, fg_ref[...]) if has_final else out

    if x_grouped or has_extra or has_final:
        @pl.when(pl.program_id(0) < npb)
        def _():
            compute(x_refs[0], e_refs[0], o_refs[0])

        @pl.when(pl.program_id(0) >= npb)
        def _():
            compute(x_refs[1], e_refs[1], o_refs[1])
    else:
        compute(x_refs[0], None, o_refs[0])


def _ffn(x, norm_g, w_in_all, w_out_all, layer, which, tp, extra=None, w_extra=None, final_g=None):
    x_grouped = isinstance(x, tuple)
    t = x[0].shape[0] + x[1].shape[0] if x_grouped else x.shape[0]
    has_extra = extra is not None
    has_final = final_g is not None
    tm = _pick(math.gcd(tp, t - tp), (512, 256, 128, 64, 32, 16, 8))
    npb = tp // tm
    row = lambda i: (i, 0)
    prow, srow = _group_row_maps(npb)
    if x_grouped:
        args = [x[0], x[1]]
        specs = [pl.BlockSpec((tm, D_MODEL), prow), pl.BlockSpec((tm, D_MODEL), srow)]
    else:
        args, specs = [x], [pl.BlockSpec((tm, D_MODEL), row)]
    if has_extra:
        args += [extra[0], extra[1], w_extra]
        specs += [pl.BlockSpec((tm, extra[0].shape[1]), prow), pl.BlockSpec((tm, extra[1].shape[1]), srow),
                  _resident_spec(w_extra.shape)]
    pick = lambda i: (layer, which, 0, 0)
    args += [norm_g.reshape(1, D_MODEL), w_in_all, w_out_all]
    specs += [_const_spec((1, D_MODEL)),
              pl.BlockSpec((None, None) + w_in_all.shape[2:], pick, pipeline_mode=pl.Buffered(1)),
              pl.BlockSpec((None, None) + w_out_all.shape[2:], pick, pipeline_mode=pl.Buffered(1))]
    if has_final:
        args.append(final_g.reshape(1, D_MODEL))
        specs.append(_const_spec((1, D_MODEL)))
        out_shape = [jax.ShapeDtypeStruct((tp, D_MODEL), F32), jax.ShapeDtypeStruct((t - tp, D_MODEL), F32)]
        out_specs = [pl.BlockSpec((tm, D_MODEL), prow), pl.BlockSpec((tm, D_MODEL), srow)]
    else:
        out_shape = [jax.ShapeDtypeStruct((t, D_MODEL), F32)]
        out_specs = [pl.BlockSpec((tm, D_MODEL), row)]
    res = pl.pallas_call(
        functools.partial(_ffn_body, x_grouped=x_grouped, has_extra=has_extra, has_final=has_final, npb=npb),
        grid=(t // tm,), in_specs=specs, out_specs=out_specs, out_shape=out_shape,
        input_output_aliases={} if (x_grouped or has_final) else {0: 0},
        compiler_params=_params(("arbitrary",) if has_final else ("parallel",)), name="ffn",
    )(*args)
    return res if has_final else res[0]


def _rope(blk, c, s1, s2):
    return blk * c + pltpu.roll(blk, ROPE_HALF, 1) * s1 + pltpu.roll(blk, LANES - ROPE_HALF, 1) * s2


def _mixin_body(x_ref, g_ref, win_ref, qn_ref, wuq_ref, kvn_ref, wuk_ref, c_ref, s1_ref, s2_ref,
                q_ref, k_ref, ckv_ref, ckvb_ref, krp_ref, misc_ref, z_ref, xbc_ref, u_ref, v_ref):
    h = _rms(x_ref[...], g_ref[...]).astype(BF16)
    p = _dot(h, win_ref[...])
    z_ref[...] = p[:, P_Z:P_XBC]
    xbc_ref[...] = p[:, P_XBC:P_U]
    u_ref[...] = p[:, P_U:P_V]
    v_ref[...] = p[:, P_V:P_MISC]
    misc = p[:, P_MISC:P_END]
    misc_ref[...] = misc
    c, s1, s2 = c_ref[...], s1_ref[...], s2_ref[...]
    krp = _rope(misc, c, s1, s2)
    krp_ref[...] = krp
    ckvn = _rms(p[:, P_CKV:P_Z], kvn_ref[...])
    ckv_ref[...] = ckvn
    ckvb = ckvn.astype(BF16)
    ckvb_ref[...] = ckvb
    kn = _dot(ckvb, wuk_ref[...])
    qn = _rms(p[:, P_CQ:P_CKV], qn_ref[...]).astype(BF16)
    q = _dot(qn, wuq_ref[...]) * (MLA_SCALE * LOG2E)
    for hh in range(MLA_HEADS):
        sl = slice(hh * HEAD_PAD, (hh + 1) * HEAD_PAD)
        q_ref[hh] = _rope(q[:, sl], c, s1, s2).astype(BF16)
        k_ref[hh] = (kn[:, sl] + krp).astype(BF16)


def _mixin_tile(tp, ts, seq):
    return _pick(math.gcd(math.gcd(tp, ts), seq), (512, 256, 128, 64, 32, 16))


def _mixin(x, lw, tabs, tp, seq):
    t = x.shape[0]
    tm = _mixin_tile(tp, t - tp, seq)
    npb, nqt = tp // tm, seq // tm
    row = lambda i: (i, 0)
    trow = lambda i: (jnp.where(i < npb, i % nqt, nqt), 0)
    hrow = lambda i: (0, i, 0)
    wide = MLA_HEADS * HEAD_PAD
    outs = [
        ((MLA_HEADS, t, HEAD_PAD), BF16, pl.BlockSpec((MLA_HEADS, tm, HEAD_PAD), hrow)),
        ((MLA_HEADS, t, HEAD_PAD), BF16, pl.BlockSpec((MLA_HEADS, tm, HEAD_PAD), hrow)),
        ((t, KV_LORA), F32, pl.BlockSpec((tm, KV_LORA), row)),
        ((t, KV_LORA), BF16, pl.BlockSpec((tm, KV_LORA), row)),
        ((t, LANES), F32, pl.BlockSpec((tm, LANES), row)),
        ((t, LANES), F32, pl.BlockSpec((tm, LANES), row)),
        ((t, D_INNER), F32, pl.BlockSpec((tm, D_INNER), row)),
        ((t, CONV_DIM), F32, pl.BlockSpec((tm, CONV_DIM), row)),
        ((t, GM_WIDTH), F32, pl.BlockSpec((tm, GM_WIDTH), row)),
        ((t, GM_WIDTH), F32, pl.BlockSpec((tm, GM_WIDTH), row)),
    ]
    return pl.pallas_call(
        _mixin_body, grid=(t // tm,),
        in_specs=[pl.BlockSpec((tm, D_MODEL), row), _const_spec((1, D_MODEL)),
                  _resident_spec((D_MODEL, P_END)), _const_spec((1, Q_LORA)),
                  _resident_spec((Q_LORA, wide)), _const_spec((1, KV_LORA)),
                  _resident_spec((KV_LORA, wide)),
                  pl.BlockSpec((tm, LANES), trow), pl.BlockSpec((tm, LANES), trow),
                  pl.BlockSpec((tm, LANES), trow)],
        out_specs=[o[2] for o in outs],
        out_shape=[jax.ShapeDtypeStruct(o[0], o[1]) for o in outs],
        compiler_params=_params(("parallel",)), name="mixin",
    )(x, lw["mix_norm"], lw["w_in_p"], lw["q_norm"], lw["w_uq_p"], lw["kv_norm"], lw["w_uk_p"], *tabs)


def _pattn_body(qi_ref, kj_ref, q_ref, k_ref, v_ref, wuv_ref, o_ref, m_ref, l_ref, acc_ref, *, tq, kvb):
    n = pl.program_id(1)
    i = qi_ref[n]
    j = kj_ref[n]
    last = (i * tq + tq - 1) // kvb

    @pl.when(j == 0)
    def _():
        m_ref[...] = jnp.full(m_ref.shape, NEG_INF, F32)
        l_ref[...] = jnp.zeros(l_ref.shape, F32)
        acc_ref[...] = jnp.zeros(acc_ref.shape, F32)

    def step(masked):
        v = v_ref[...]
        if masked:
            qpos = i * tq + lax.broadcasted_iota(jnp.int32, (tq, kvb), 0)
            kpos = j * kvb + lax.broadcasted_iota(jnp.int32, (tq, kvb), 1)
            ok = kpos <= qpos
        for hh in range(MLA_HEADS):
            s = _dot_nt(q_ref[hh], k_ref[hh])
            if masked:
                s = jnp.where(ok, s, NEG_INF)
            m_prev = m_ref[hh]
            m_new = jnp.maximum(m_prev, jnp.max(s, axis=1, keepdims=True))
            alpha = jnp.exp2(m_prev - m_new)
            p = jnp.exp2(s - _lane_tile(m_new, kvb // LANES))
            psum = p[:, :LANES]
            for t in range(1, kvb // LANES):
                psum = psum + p[:, t * LANES:(t + 1) * LANES]
            l_ref[hh] = alpha * l_ref[hh] + psum
            acc_ref[hh] = acc_ref[hh] * _lane_tile(alpha, KV_LORA // LANES) + _dot(p.astype(BF16), v)
            m_ref[hh] = m_new

    @pl.when(j == last)
    def _():
        step(True)
        for hh in range(MLA_HEADS):
            o = (acc_ref[hh] / jnp.sum(l_ref[hh], axis=1, keepdims=True)).astype(BF16)
            o_ref[:, hh * V_HEAD:(hh + 1) * V_HEAD] = _dot(o, wuv_ref[hh]).astype(BF16)

    @pl.when(j != last)
    def _():
        step(False)


def _prompt_attention(q, k, vb, w_uv, bsz, seq):
    tq = _pick(seq, (512, 256, 128))
    kvb = _pick(seq, (512, 256, 128))
    nq, nkv = seq // tq, seq // kvb
    pairs = [(i, j) for i in range(nq) for j in range((i * tq + tq - 1) // kvb + 1)]
    qi = jnp.asarray(np.array([p[0] for p in pairs], np.int32))
    kj = jnp.asarray(np.array([p[1] for p in pairs], np.int32))
    grid_spec = pltpu.PrefetchScalarGridSpec(
        num_scalar_prefetch=2, grid=(bsz, len(pairs)),
        in_specs=[
            pl.BlockSpec((MLA_HEADS, tq, HEAD_PAD), lambda b, n, qi, kj: (0, b * nq + qi[n], 0)),
            pl.BlockSpec((MLA_HEADS, kvb, HEAD_PAD), lambda b, n, qi, kj: (0, b * nkv + kj[n], 0)),
            pl.BlockSpec((kvb, KV_LORA), lambda b, n, qi, kj: (b * nkv + kj[n], 0)),
            pl.BlockSpec((MLA_HEADS, KV_LORA, V_HEAD), lambda b, n, qi, kj: (0, 0, 0)),
        ],
        out_specs=pl.BlockSpec((tq, MLA_HEADS * V_HEAD), lambda b, n, qi, kj: (b * nq + qi[n], 0)),
        scratch_shapes=[pltpu.VMEM((MLA_HEADS, tq, LANES), F32), pltpu.VMEM((MLA_HEADS, tq, LANES), F32),
                        pltpu.VMEM((MLA_HEADS, tq, KV_LORA), F32)],
    )
    return pl.pallas_call(
        functools.partial(_pattn_body, tq=tq, kvb=kvb), grid_spec=grid_spec,
        out_shape=jax.ShapeDtypeStruct((bsz * seq, MLA_HEADS * V_HEAD), BF16),
        compiler_params=_params(("parallel", "arbitrary")), name="prompt_attn",
    )(qi, kj, q, k, vb, w_uv)


def _sattn_body(pt_ref, q_ref, ckvn_ref, krn_ref, wukt_ref, wuv_ref, ckv_hbm, ckr_hbm, o_ref,
                kbuf, rbuf, sem, qlat_ref, qrope_ref, m_ref, l_ref, acc_ref, *, layer, pc, n_seq, n_ch, page, nsub):
    b = pl.program_id(0)
    c = pl.program_id(1)
    step = b * n_ch + c
    slot = step % 2
    n_new = q_ref.shape[1] // MLA_HEADS

    def copies(bb, cc, sl):
        kv, kr = [], []
        for pg in range(pc):
            pid = pt_ref[bb, cc * pc + pg]
            kv.append(pltpu.make_async_copy(ckv_hbm.at[layer, pid], kbuf.at[sl, pl.ds(pg * page, page)], sem.at[0, sl]))
            kr.append(pltpu.make_async_copy(ckr_hbm.at[layer, pid], rbuf.at[sl, pg], sem.at[1, sl]))
        return kv + kr

    def start_all(cps):
        for t, cp in enumerate(cps):
            cp.start(priority=t % 2)

    @pl.when(step == 0)
    def _():
        start_all(copies(b, c, slot))

    @pl.when(step + 1 < n_seq * n_ch)
    def _():
        wrap = c + 1 == n_ch
        nb = jnp.where(wrap, b + 1, b)
        nc = jnp.where(wrap, 0, c + 1)
        start_all(copies(nb, nc, 1 - slot))

    @pl.when(c == 0)
    def _():
        m_ref[...] = jnp.full(m_ref.shape, NEG_INF, F32)
        l_ref[...] = jnp.zeros(l_ref.shape, F32)
        acc_ref[...] = jnp.zeros(acc_ref.shape, F32)
        q32 = q_ref[0].astype(F32)
        for hh in range(MLA_HEADS):
            qh = q32[hh * n_new:(hh + 1) * n_new, :].astype(BF16)
            qlat_ref[hh * n_new:(hh + 1) * n_new, :] = _dot(qh, wukt_ref[hh])
        qrope_ref[...] = q32[:, ROPE_LO:ROPE_LO + QK_ROPE]

    def merge(parts):
        m_prev = m_ref[...]
        m_new = m_prev
        for m, _, _ in parts:
            m_new = jnp.maximum(m_new, m)
        alpha = jnp.exp2(m_prev - m_new)
        l_new = alpha * l_ref[...]
        acc = acc_ref[...] * alpha
        for m, l, a in parts:
            w = jnp.exp2(m - m_new)
            l_new = l_new + w * l
            acc = acc + w * a
        m_ref[...] = m_new
        l_ref[...] = l_new
        acc_ref[...] = acc

    for cp in copies(b, c, slot):
        cp.wait()

    def chunk(last):
        qlat = qlat_ref[...].astype(BF16)
        qrope = qrope_ref[...].astype(BF16)
        pps = pc // nsub
        vals, scores = [], []
        for u in range(nsub):
            kc = kbuf[slot, u * pps * page:(u + 1) * pps * page, :].astype(BF16)
            rc = jnp.concatenate([rbuf[slot, u * pps + pg] for pg in range(pps)], axis=1).astype(BF16)
            vals.append(kc)
            scores.append(_dot_nt(qlat, kc) + _dot(qrope, rc))
        if last:
            pad = LANES - n_new
            kn = jnp.concatenate([ckvn_ref[...], jnp.zeros((pad, KV_LORA), F32)], axis=0).astype(BF16)
            rn = jnp.concatenate([krn_ref[...][:, ROPE_LO:ROPE_LO + QK_ROPE], jnp.zeros((pad, QK_ROPE), F32)],
                                 axis=0).astype(BF16)
            s = _dot_nt(qlat, kn) + _dot_nt(qrope, rn)
            tpos = lax.broadcasted_iota(jnp.int32, s.shape, 0) % n_new
            kpos = lax.broadcasted_iota(jnp.int32, s.shape, 1)
            vals.append(kn)
            scores.append(jnp.where(kpos <= tpos, s, NEG_INF))
        maxes = [jnp.max(s, axis=1, keepdims=True) for s in scores]
        probs = [jnp.exp2(s - m) for s, m in zip(scores, maxes)]
        sums = [jnp.sum(p, axis=1, keepdims=True) for p in probs]
        outs = [_dot(p.astype(BF16), v) for p, v in zip(probs, vals)]
        merge(list(zip(maxes, sums, outs)))
        if last:
            o = acc_ref[...] / l_ref[...]
            pieces = [_dot(o[hh * n_new:(hh + 1) * n_new, :].astype(BF16), wuv_ref[hh]) for hh in range(MLA_HEADS)]
            o_ref[0] = jnp.concatenate(pieces, axis=1).astype(BF16)

    @pl.when(c == n_ch - 1)
    def _():
        chunk(True)

    @pl.when(c != n_ch - 1)
    def _():
        chunk(False)


SATTN_PAGES = 64
SATTN_SPLIT = 2


def _sample_attention(q_s, ckv_all, krp_all, w_ukt_p, w_uv, cache_kv, cache_kr_t, page_table, layer, tp, n_new):
    n_seq, n_pages = page_table.shape
    page = cache_kv.shape[2]
    pc = _pick(n_pages, (SATTN_PAGES, 16, 8, 4, 2, 1))
    nsub = SATTN_SPLIT if pc % SATTN_SPLIT == 0 else 1
    n_ch = n_pages // pc
    rows = MLA_HEADS * n_new
    off = tp // n_new
    grid_spec = pltpu.PrefetchScalarGridSpec(
        num_scalar_prefetch=1, grid=(n_seq, n_ch),
        in_specs=[
            pl.BlockSpec((1, rows, HEAD_PAD), lambda b, c, pt: (b, 0, 0)),
            pl.BlockSpec((n_new, KV_LORA), lambda b, c, pt: (off + b, 0)),
            pl.BlockSpec((n_new, LANES), lambda b, c, pt: (off + b, 0)),
            pl.BlockSpec((MLA_HEADS, HEAD_PAD, KV_LORA), lambda b, c, pt: (0, 0, 0)),
            pl.BlockSpec((MLA_HEADS, KV_LORA, V_HEAD), lambda b, c, pt: (0, 0, 0)),
            pl.BlockSpec(memory_space=pl.ANY),
            pl.BlockSpec(memory_space=pl.ANY),
        ],
        out_specs=pl.BlockSpec((1, n_new, MLA_HEADS * V_HEAD), lambda b, c, pt: (b, 0, 0)),
        scratch_shapes=[
            pltpu.VMEM((2, pc * page, KV_LORA), F32), pltpu.VMEM((2, pc, QK_ROPE, page), F32),
            pltpu.SemaphoreType.DMA((2, 2)),
            pltpu.VMEM((rows, KV_LORA), F32), pltpu.VMEM((rows, QK_ROPE), F32),
            pltpu.VMEM((rows, 1), F32), pltpu.VMEM((rows, 1), F32), pltpu.VMEM((rows, KV_LORA), F32),
        ],
    )
    out = pl.pallas_call(
        functools.partial(_sattn_body, layer=layer, pc=pc, n_seq=n_seq, n_ch=n_ch, page=page, nsub=nsub),
        grid_spec=grid_spec,
        out_shape=jax.ShapeDtypeStruct((n_seq, n_new, MLA_HEADS * V_HEAD), BF16),
        compiler_params=_params(("arbitrary", "arbitrary")), name="sample_attn",
    )(page_table, q_s, ckv_all, krp_all, w_ukt_p, w_uv, cache_kv, cache_kr_t)
    return out.reshape(n_seq * n_new, MLA_HEADS * V_HEAD)


def _ssd_consts(seg):
    r = np.arange(CHUNK)
    same = (r[:, None] // seg) == (r[None, :] // seg)
    caus = (same & (r[None, :] <= r[:, None])).astype(np.float32)
    plast = (r[None, :] == (r[:, None] // seg) * seg + seg - 1).astype(np.float32)
    et = np.zeros((D_INNER, LANES), np.float32)
    et[np.arange(D_INNER), MISC_DT + np.arange(D_INNER) // SSM_HEAD_DIM] = 1.0
    return jnp.asarray(caus), jnp.asarray(plast), jnp.asarray(et)


def _ssd_tile(act, z, misc, caus, plast, et, dtb, alog, dskip, ng, h_in_fn, seg):
    nseg = CHUNK // seg
    lane = lax.broadcasted_iota(jnp.int32, (1, LANES), 1)
    hmask = (lane >= MISC_DT) & (lane < MISC_DT + SSM_HEADS)
    lo = lax.broadcasted_iota(jnp.int32, (CHUNK, LANES), 1) < SSM_HEAD_DIM
    xs = act[:, :D_INNER]
    bm = act[:, D_INNER:D_INNER + LANES]
    cm = act[:, D_INNER + LANES:D_INNER + 2 * LANES]
    dtv = jnp.where(hmask, jax.nn.softplus(misc + dtb), 0.0)
    a_row = jnp.where(hmask, -jnp.exp(alog), 0.0)
    acs = _dot_exact(caus, dtv * a_row)
    acs_last = _dot_exact(plast, acs)
    e_acs = jnp.exp(acs)
    to_end = jnp.exp(acs_last - acs)
    e_last = jnp.exp(acs_last)
    acs_t = acs.T
    dec_all = _dot_exact(et, e_last, NT_DIMS)

    bm_b = bm.astype(BF16)
    cm_g = [jnp.where(lo, cm, 0.0).astype(BF16), jnp.where(lo, 0.0, cm).astype(BF16)]
    g_mat = [_dot_nt(cm_g[g], bm_b) for g in range(SSM_GROUPS)]
    cm_half = [cm[:, :D_STATE].astype(BF16), cm[:, D_STATE:].astype(BF16)]

    def col(arr, hh):
        return arr[:, MISC_DT + hh:MISC_DT + hh + 1]

    def pair(arr, pb):
        return jnp.where(lo, col(arr, 2 * pb), col(arr, 2 * pb + 1))

    hpg = SSM_HEADS // SSM_GROUPS
    rows_g = hpg * SSM_HEAD_DIM
    y_off = []
    for g in range(SSM_GROUPS):
        parts = []
        for j in range(nseg):
            hj = h_in_fn(j)[g * rows_g:(g + 1) * rows_g, :].astype(BF16)
            parts.append(_dot_nt(cm_half[g][j * seg:(j + 1) * seg, :], hj))
        y_off.append(parts[0] if nseg == 1 else jnp.concatenate(parts, axis=0))

    y_blocks, xte_blocks = [], []
    for pb in range(SSM_HEADS // 2):
        g = (2 * pb) // hpg
        sl = slice(pb * LANES, (pb + 1) * LANES)
        xs_p = xs[:, sl]
        xdt = xs_p * pair(dtv, pb)
        xte_blocks.append((xdt * pair(to_end, pb)).astype(BF16))
        xdt_b = xdt.astype(BF16)
        yd = []
        for hh in (2 * pb, 2 * pb + 1):
            segm = col(acs, hh) - acs_t[MISC_DT + hh:MISC_DT + hh + 1, :]
            decay = jnp.exp(jnp.where(caus > 0.0, segm, NEG_INF))
            yd.append(_dot((g_mat[g] * decay).astype(BF16), xdt_b))
        y_d = jnp.where(lo, yd[0], yd[1])
        off_sl = slice((pb % (hpg // 2)) * LANES, (pb % (hpg // 2) + 1) * LANES)
        y_o = y_off[g][:, off_sl] * pair(e_acs, pb)
        y_blocks.append(y_d + y_o + dskip[:, sl] * xs_p)
    y = jnp.concatenate(y_blocks, axis=1)
    yn = _rms(y * _silu(z), ng)

    xte = jnp.concatenate(xte_blocks, axis=1)
    rowi = lax.broadcasted_iota(jnp.int32, (D_INNER, D_STATE), 0) < rows_g
    new_states = []
    if nseg == 1:
        s_all = lax.dot_general(xte, bm_b, TN_DIMS, preferred_element_type=F32)
        s_sel = jnp.where(rowi, s_all[:, :D_STATE], s_all[:, D_STATE:])
        new_states.append(h_in_fn(0) * dec_all[:, 0:1] + s_sel)
    else:
        tr = lax.broadcasted_iota(jnp.int32, (CHUNK, LANES), 0) // seg
        bbd = jnp.concatenate([jnp.where(tr == j, bm, 0.0) for j in range(nseg)], axis=1).astype(BF16)
        s_all = lax.dot_general(xte, bbd, TN_DIMS, preferred_element_type=F32)
        for j in range(nseg):
            sj = s_all[:, j * LANES:(j + 1) * LANES]
            s_sel = jnp.where(rowi, sj[:, :D_STATE], sj[:, D_STATE:])
            new_states.append(h_in_fn(j) * dec_all[:, j * seg:j * seg + 1] + s_sel)
    return yn, new_states


def _ssd_prompt_body(*refs, bsz):
    zs, xbcs, miscs = refs[:bsz], refs[bsz:2 * bsz], refs[2 * bsz:3 * bsz]
    (caus_ref, plast_ref, et_ref, cw_ref, cb_ref, dtb_ref, alog_ref, dskip_ref, ng_ref,
     y_ref, st_ref, xpad_ref) = refs[3 * bsz:]
    c = pl.program_id(0)
    hist = CONV_K - 1

    @pl.when(c == 0)
    def _():
        xpad_ref[:, 0:8, :] = jnp.zeros((bsz, 8, CONV_DIM), F32)
        st_ref[...] = jnp.zeros(st_ref.shape, F32)

    for b in range(bsz):
        x = xbcs[b][...]
        xpad_ref[b, 8:8 + CHUNK, :] = x
        conv = cb_ref[...] + x * cw_ref[hist:hist + 1, :]
        for k in range(hist):
            conv = conv + xpad_ref[b, 8 - hist + k:8 - hist + k + CHUNK, :] * cw_ref[k:k + 1, :]
        xpad_ref[b, 0:8, :] = x[CHUNK - 8:, :]
        act = _silu(conv)
        yn, new = _ssd_tile(act, zs[b][...], miscs[b][...], caus_ref[...], plast_ref[...], et_ref[...],
                            dtb_ref[...], alog_ref[...], dskip_ref[...], ng_ref[...], lambda j, b=b: st_ref[b], CHUNK)
        y_ref[b] = yn.astype(BF16)
        st_ref[b] = new[0]


def _ssd_sample_body(z_ref, xbc_ref, misc_ref, sp_ref, h0_ref, caus_ref, plast_ref, et_ref, cw_ref, cb_ref,
                     dtb_ref, alog_ref, dskip_ref, ng_ref, y_ref, st_ref, *, seg):
    hist = CONV_K - 1
    x = xbc_ref[...]
    sp = sp_ref[...]
    tpos = lax.broadcasted_iota(jnp.int32, (CHUNK, CONV_DIM), 0) % seg
    conv = cb_ref[...] + x * cw_ref[hist:hist + 1, :]
    for j in range(1, CONV_K):
        xin = pltpu.roll(x, j, 0)
        up = hist - j
        sin = sp if up == 0 else pltpu.roll(sp, CHUNK - up, 0)
        conv = conv + jnp.where(tpos >= j, xin, sin) * cw_ref[hist - j:hist - j + 1, :]
    act = _silu(conv)
    yn, new = _ssd_tile(act, z_ref[...], misc_ref[...], caus_ref[...], plast_ref[...], et_ref[...],
                        dtb_ref[...], alog_ref[...], dskip_ref[...], ng_ref[...], lambda j: h0_ref[j], seg)
    y_ref[...] = yn.astype(BF16)
    for j in range(len(new)):
        st_ref[j] = new[j]


def _ssd_param_args(lw):
    return [lw["conv_w"], lw["conv_b"], lw["dtb_row"], lw["alog_row"], lw["dskip_exp"], lw["ssm_norm"]]


def _ssd_param_specs():
    return [_const_spec((CONV_K, CONV_DIM)), _const_spec((1, CONV_DIM)), _const_spec((1, LANES)),
            _const_spec((1, LANES)), _const_spec((1, D_INNER)), _const_spec((1, D_INNER))]


def _ssd_prompt(z, xbc, misc, lw, bsz, seq):
    nc = seq // CHUNK
    caus, plast, et = _ssd_consts(CHUNK)
    rows = [(lambda c, b=b: (b * nc + c, 0)) for b in range(bsz)]
    y, st = pl.pallas_call(
        functools.partial(_ssd_prompt_body, bsz=bsz), grid=(nc,),
        in_specs=[pl.BlockSpec((CHUNK, D_INNER), r) for r in rows]
        + [pl.BlockSpec((CHUNK, CONV_DIM), r) for r in rows] + [pl.BlockSpec((CHUNK, LANES), r) for r in rows]
        + [_const_spec((CHUNK, CHUNK)), _const_spec((CHUNK, CHUNK)), _const_spec((D_INNER, LANES))]
        + _ssd_param_specs(),
        out_specs=[pl.BlockSpec((bsz, CHUNK, D_INNER), lambda c: (0, c, 0)),
                   pl.BlockSpec((bsz, D_INNER, D_STATE), lambda c: (0, 0, 0))],
        out_shape=[jax.ShapeDtypeStruct((bsz, seq, D_INNER), BF16),
                   jax.ShapeDtypeStruct((bsz, D_INNER, D_STATE), F32)],
        scratch_shapes=[pltpu.VMEM((bsz, 8 + CHUNK, CONV_DIM), F32)],
        compiler_params=_params(("arbitrary",)), name="ssd_prompt",
    )(*([z] * bsz), *([xbc] * bsz), *([misc] * bsz), caus, plast, et, *_ssd_param_args(lw))
    return y.reshape(bsz * seq, D_INNER), st


def _ssd_sample(z, xbc, misc, sp, h0, lw, tp, n_seq, seg):
    nseg = CHUNK // seg
    nt = n_seq // nseg
    off = tp // CHUNK
    caus, plast, et = _ssd_consts(seg)
    row = lambda i: (off + i, 0)
    y, st = pl.pallas_call(
        functools.partial(_ssd_sample_body, seg=seg), grid=(nt,),
        in_specs=[pl.BlockSpec((CHUNK, D_INNER), row), pl.BlockSpec((CHUNK, CONV_DIM), row),
                  pl.BlockSpec((CHUNK, LANES), row), pl.BlockSpec((CHUNK, CONV_DIM), lambda i: (i, 0)),
                  pl.BlockSpec((nseg, D_INNER, D_STATE), lambda i: (i, 0, 0)),
                  _const_spec((CHUNK, CHUNK)), _const_spec((CHUNK, CHUNK)),
                  _const_spec((D_INNER, LANES))] + _ssd_param_specs(),
        out_specs=[pl.BlockSpec((CHUNK, D_INNER), lambda i: (i, 0)),
                   pl.BlockSpec((nseg, D_INNER, D_STATE), lambda i: (i, 0, 0))],
        out_shape=[jax.ShapeDtypeStruct((n_seq * seg, D_INNER), BF16),
                   jax.ShapeDtypeStruct((n_seq, D_INNER, D_STATE), F32)],
        compiler_params=_params(("parallel",)), name="ssd_sample",
    )(z, xbc, misc, sp, h0, caus, plast, et, *_ssd_param_args(lw))
    return y, st


def _gmlp_body(u_ref, v_ref, w_ref, mask_ref, b_ref, ng_ref, o_ref, *vn_out, reps):
    lo = lax.broadcasted_iota(jnp.int32, (CHUNK, LANES), 1) < GM_GROUP_DIM
    mask = mask_ref[...] > 0.0
    wm = [jnp.where(mask, w_ref[g], 0.0).astype(BF16) for g in range(GM_GROUPS)]
    bias = b_ref[...]
    for r in range(reps):
        rs = slice(r * CHUNK, (r + 1) * CHUNK)
        ug = jax.nn.gelu(u_ref[rs, :])
        vn = _rms(jax.nn.gelu(v_ref[rs, :]), ng_ref[...])
        if vn_out:
            vn_out[0][rs, :] = vn
        vb = vn.astype(BF16)
        for pb in range(GM_GROUPS // 2):
            sl = slice(pb * LANES, (pb + 1) * LANES)
            s = jnp.where(lo, _dot(wm[2 * pb], vb[:, sl]), _dot(wm[2 * pb + 1], vb[:, sl])) + bias[:, sl]
            o_ref[rs, sl] = (ug[:, sl] * s).astype(BF16)


def _gmlp(u, v, w_sp, mask, bias_exp, norm_g, row_off, n_rows, want_vn):
    reps = _pick(n_rows // CHUNK, (4, 2, 1))
    tm = reps * CHUNK
    off = row_off // tm
    row = lambda i: (off + i, 0)
    out_shape = [jax.ShapeDtypeStruct((n_rows, GM_WIDTH), BF16)]
    out_specs = [pl.BlockSpec((tm, GM_WIDTH), lambda i: (i, 0))]
    if want_vn:
        out_shape.append(jax.ShapeDtypeStruct((n_rows, GM_WIDTH), F32))
        out_specs.append(pl.BlockSpec((tm, GM_WIDTH), lambda i: (i, 0)))
    return pl.pallas_call(
        functools.partial(_gmlp_body, reps=reps), grid=(n_rows // tm,),
        in_specs=[pl.BlockSpec((tm, GM_WIDTH), row), pl.BlockSpec((tm, GM_WIDTH), row),
                  _const_spec((GM_GROUPS, CHUNK, CHUNK)), _const_spec((CHUNK, CHUNK)),
                  _const_spec((CHUNK, GM_WIDTH)), _const_spec((1, GM_WIDTH))],
        out_specs=out_specs, out_shape=out_shape,
        compiler_params=_params(("parallel",)), name="gmlp",
    )(u, v, w_sp, mask, bias_exp, norm_g)


def _merge_body(x_ref, ohp_ref, ysp_ref, gmp_ref, ohs_ref, yss_ref, gms_ref, ng_ref, wg_ref, bg_ref, womla_ref,
                wossm_ref, wogm_ref, wout_ref, mqn_ref, wmq_ref, xo_ref, qm_ref, *, npb):
    def compute(oh_ref, ys_ref, gm_ref):
        x = x_ref[...]
        h = _rms(x, ng_ref[...]).astype(BF16)
        mix = None
        for br, (b_ref, w_ref) in enumerate(((oh_ref, womla_ref), (ys_ref, wossm_ref), (gm_ref, wogm_ref))):
            sl = slice(br * D_MODEL, (br + 1) * D_MODEL)
            gate = 1.0 / (1.0 + jnp.exp(-(_dot(h, wg_ref[:, sl]) + bg_ref[:, sl])))
            term = gate * _dot(b_ref[...], w_ref[...])
            mix = term if mix is None else mix + term
        x2 = x + _dot(mix.astype(BF16), wout_ref[...])
        xo_ref[...] = x2
        hq = _rms(x2, mqn_ref[...]).astype(BF16)
        qm_ref[...] = _dot(hq, wmq_ref[...])

    @pl.when(pl.program_id(0) < npb)
    def _():
        compute(ohp_ref, ysp_ref, gmp_ref)

    @pl.when(pl.program_id(0) >= npb)
    def _():
        compute(ohs_ref, yss_ref, gms_ref)


def _group_row_maps(npb):
    return (lambda i: (jnp.minimum(i, npb - 1), 0)), (lambda i: (jnp.maximum(i - npb, 0), 0))


def _merge(x, branches_p, branches_s, lw, tp):
    t = x.shape[0]
    tm = _pick(math.gcd(tp, t - tp), (512, 256, 128, 64, 32, 16))
    row = lambda i: (i, 0)
    prow, srow = _group_row_maps(tp // tm)
    half = MLA_HEADS * V_HEAD
    widths = (half, D_INNER, GM_WIDTH)
    return pl.pallas_call(
        functools.partial(_merge_body, npb=tp // tm), grid=(t // tm,),
        in_specs=[pl.BlockSpec((tm, D_MODEL), row)]
        + [pl.BlockSpec((tm, w), prow) for w in widths] + [pl.BlockSpec((tm, w), srow) for w in widths]
        + [_const_spec((1, D_MODEL)), _resident_spec((D_MODEL, 3 * D_MODEL)), _const_spec((1, 3 * D_MODEL)),
           _resident_spec((half, D_MODEL)), _resident_spec((D_INNER, D_MODEL)),
           _resident_spec((GM_WIDTH, D_MODEL)), _resident_spec((D_MODEL, D_MODEL)),
           _const_spec((1, D_MODEL)), _resident_spec((D_MODEL, MEM_WIDTH))],
        out_specs=[pl.BlockSpec((tm, D_MODEL), row), pl.BlockSpec((tm, MEM_WIDTH), row)],
        out_shape=[jax.ShapeDtypeStruct((t, D_MODEL), F32), jax.ShapeDtypeStruct((t, MEM_WIDTH), F32)],
        input_output_aliases={0: 0}, compiler_params=_params(("parallel",)), name="merge",
    )(x, *branches_p, *branches_s, lw["mix_norm"], lw["w_gate"], lw["b_gate"], lw["w_o_mla"], lw["w_o_ssm"],
      lw["w_o_gm"], lw["w_out"], lw["mem_q_norm"], lw["w_mem_q"])


def _memattn_body(q_ref, k_ref, v_ref, o_ref, *, sb, feat_major):
    q = q_ref[...]
    rows = q.shape[0] // sb
    if feat_major:
        m = k_ref.shape[2]
        kb = jnp.concatenate([k_ref[e] for e in range(sb)], axis=1).astype(BF16)
        vb = jnp.concatenate([v_ref[e] for e in range(sb)], axis=1).astype(BF16)
    else:
        m = k_ref.shape[1]
        kb = k_ref[...].reshape(sb * m, MEM_WIDTH).astype(BF16)
        vb = v_ref[...].reshape(sb * m, MEM_WIDTH).astype(BF16)
    lane = lax.broadcasted_iota(jnp.int32, q.shape, 1) // MEM_HEAD_DIM
    if sb > 1:
        own = (lax.broadcasted_iota(jnp.int32, (q.shape[0], sb * m), 0) // rows
               == lax.broadcasted_iota(jnp.int32, (q.shape[0], sb * m), 1) // m)
    out = jnp.zeros(q.shape, F32)
    for hh in range(MEM_HEADS):
        sel = lane == hh
        qh = jnp.where(sel, q, 0.0).astype(BF16)
        s = (_dot(qh, kb) if feat_major else _dot_nt(qh, kb)) * MEM_SCALE
        if sb > 1:
            s = jnp.where(own, s, NEG_INF)
        p = jnp.exp(s - jnp.max(s, axis=1, keepdims=True))
        p = (p / jnp.sum(p, axis=1, keepdims=True)).astype(BF16)
        out = out + jnp.where(sel, _dot_nt(p, vb) if feat_major else _dot(p, vb), 0.0)
    o_ref[...] = out


MEMATTN_BATCH = 8


def _mem_attention(qm, mem_k, mem_v, row_off, nb, rows_per_b, layer=None):
    tm = _pick(rows_per_b, (512, 256, 128, 64, 32, 16, 8))
    nt = rows_per_b // tm
    sb = MEMATTN_BATCH if (nt == 1 and nb % MEMATTN_BATCH == 0 and tm * MEMATTN_BATCH <= 512) else 1
    off = row_off // (tm * sb)
    if layer is None:
        kv_spec = pl.BlockSpec((sb,) + mem_k.shape[1:], lambda b, i: (b, 0, 0))
    else:
        kv_spec = pl.BlockSpec((None, sb) + mem_k.shape[2:], lambda b, i: (layer, b, 0, 0))
    return pl.pallas_call(
        functools.partial(_memattn_body, sb=sb, feat_major=layer is not None), grid=(nb // sb, nt),
        in_specs=[pl.BlockSpec((tm * sb, MEM_WIDTH), lambda b, i: (off + b * nt + i, 0)), kv_spec, kv_spec],
        out_specs=pl.BlockSpec((tm * sb, MEM_WIDTH), lambda b, i: (b * nt + i, 0)),
        out_shape=jax.ShapeDtypeStruct((nb * rows_per_b, MEM_WIDTH), F32),
        compiler_params=_params(("parallel", "parallel")), name="mem_attn",
    )(qm, mem_k, mem_v)


def _memkv_body(m_ref, g_ref, w_ref, o_ref):
    o_ref[...] = _dot(_rms(m_ref[...], g_ref[...]).astype(BF16), w_ref[...])


def _memory_kv(mem2d, norm_g, w_kv):
    t = mem2d.shape[0]
    tm = _pick(t, (256, 128, 64, 32, 16, 8))
    return pl.pallas_call(
        _memkv_body, grid=(t // tm,),
        in_specs=[pl.BlockSpec((tm, D_MODEL), lambda i: (i, 0)), _const_spec((1, D_MODEL)),
                  _const_spec((D_MODEL, 2 * MEM_WIDTH))],
        out_specs=pl.BlockSpec((tm, 2 * MEM_WIDTH), lambda i: (i, 0)),
        out_shape=jax.ShapeDtypeStruct((t, 2 * MEM_WIDTH), F32),
        compiler_params=_params(("parallel",)), name="mem_kv",
    )(mem2d, norm_g, w_kv)


def _layer_weights(l, p, seg):
    w_in = p["w_in"][l]
    offs = np.cumsum((0,) + IN_SIZES)
    cq, ckv, kr, z, xbc, dt, u, v = [w_in[:, offs[i]:offs[i + 1]] for i in range(len(IN_SIZES))]
    zc = lambda n: jnp.zeros((D_MODEL, n), F32)
    w_in_p = jnp.concatenate(
        [cq, ckv, z, xbc, u, v, zc(ROPE_LO), kr, dt, zc(LANES - ROPE_LO - QK_ROPE - SSM_HEADS)], axis=1)
    hd = QK_NOPE + QK_ROPE
    w_uq_p = jnp.pad(p["w_uq"][l].reshape(Q_LORA, MLA_HEADS, hd), ((0, 0), (0, 0), (0, HEAD_PAD - hd)))
    w_uk = p["w_uk"][l]
    w_uk_p = jnp.pad(jnp.transpose(w_uk, (1, 0, 2)), ((0, 0), (0, 0), (0, HEAD_PAD - QK_NOPE)))
    w_ukt_p = jnp.pad(jnp.transpose(w_uk, (0, 2, 1)), ((0, 0), (0, HEAD_PAD - QK_NOPE), (0, 0)))
    lane_pad = lambda a: jnp.pad(a.reshape(1, SSM_HEADS), ((0, 0), (MISC_DT, LANES - MISC_DT - SSM_HEADS)))
    w_sp = p["w_spatial"][l]
    b_sp = p["b_spatial"][l]
    bias_p = jnp.repeat(b_sp.T, GM_GROUP_DIM, axis=1)
    reps = CHUNK // seg
    w_sp_s = jnp.tile(w_sp[:, :seg, :seg], (1, reps, reps))
    bias_s = jnp.tile(jnp.repeat(b_sp[:, :seg].T, GM_GROUP_DIM, axis=1), (reps, 1))
    row = lambda a: a.reshape(1, -1)
    return {
        "mix_norm": row(p["mix_norm"][l]), "w_in_p": w_in_p.astype(BF16),
        "q_norm": row(p["q_norm"][l]), "w_uq_p": w_uq_p.reshape(Q_LORA, MLA_HEADS * HEAD_PAD).astype(BF16),
        "kv_norm": row(p["kv_norm"][l]), "w_uk_p": w_uk_p.reshape(KV_LORA, MLA_HEADS * HEAD_PAD).astype(BF16),
        "w_ukt_p": w_ukt_p.astype(BF16), "w_uv": p["w_uv"][l].astype(BF16),
        "w_o_mla": p["w_o_mla"][l].astype(BF16),
        "conv_w": p["conv_w"][l], "conv_b": row(p["conv_b"][l]),
        "dtb_row": lane_pad(p["dt_bias"][l]), "alog_row": lane_pad(p["a_log"][l]),
        "dskip_exp": row(jnp.repeat(p["d_skip"][l], SSM_HEAD_DIM)), "ssm_norm": row(p["ssm_norm"][l]),
        "w_o_ssm": p["w_o_ssm"][l].astype(BF16),
        "gm_norm": row(p["gm_norm"][l]), "w_sp_p": w_sp, "bias_p": bias_p, "w_sp_s": w_sp_s, "bias_s": bias_s,
        "w_o_gm": p["w_o_gm"][l].astype(BF16),
        "w_gate": p["w_gate"][l].astype(BF16), "b_gate": row(p["b_gate"][l]), "w_out": p["w_out"][l].astype(BF16),
        "mem_q_norm": row(p["mem_q_norm"][l]), "mem_kv_norm": row(p["mem_kv_norm"][l]),
        "w_mem_q": p["w_mem_q"][l].astype(BF16), "w_mem_kv": p["w_mem_kv"][l].astype(BF16),
        "w_mem_o": p["w_mem_o"][l].astype(BF16),
    }


def _rope_tables(pos):
    inv = ROPE_THETA ** (-jnp.arange(ROPE_HALF, dtype=F32) / ROPE_HALF)
    ang = pos.astype(F32)[:, None] * inv[None, :]
    cos, sin = jnp.cos(ang), jnp.sin(ang)
    n = pos.shape[0]
    zero = lambda w: jnp.zeros((n, w), F32)
    tail = LANES - ROPE_LO - QK_ROPE
    c = jnp.concatenate([jnp.ones((n, ROPE_LO), F32), cos, cos, zero(tail)], axis=1)
    s1 = jnp.concatenate([zero(ROPE_LO + ROPE_HALF), sin, zero(tail)], axis=1)
    s2 = jnp.concatenate([zero(ROPE_LO), -sin, zero(ROPE_HALF + tail)], axis=1)
    return c, s1, s2


def _spatial_masks(seg):
    r = np.arange(CHUNK)
    same = (r[:, None] // seg) == (r[None, :] // seg)
    return jnp.asarray((same & (r[None, :] <= r[:, None])).astype(np.float32))


def kernel(x_prompt, x_sample, mem_prompt, cache_kv_latent, cache_k_rope, state_ssm, state_conv, cache_mem_k, cache_mem_v, page_table, ffn_norm, w_ffn_in, w_ffn_out, mix_norm, w_in, q_norm, w_uq, kv_norm, w_uk, w_uv, w_o_mla, conv_w, conv_b, dt_bias, a_log, d_skip, ssm_norm, w_o_ssm, gm_norm, w_spatial, b_spatial, w_o_gm, w_gate, b_gate, w_out, mem_q_norm, mem_kv_norm, w_mem_q, w_mem_kv, w_mem_o, final_norm):
    params = dict(ffn_norm=ffn_norm, w_ffn_in=w_ffn_in, w_ffn_out=w_ffn_out, mix_norm=mix_norm, w_in=w_in,
                  q_norm=q_norm, w_uq=w_uq, kv_norm=kv_norm, w_uk=w_uk, w_uv=w_uv, w_o_mla=w_o_mla,
                  conv_w=conv_w, conv_b=conv_b, dt_bias=dt_bias, a_log=a_log, d_skip=d_skip, ssm_norm=ssm_norm,
                  w_o_ssm=w_o_ssm, gm_norm=gm_norm, w_spatial=w_spatial, b_spatial=b_spatial, w_o_gm=w_o_gm,
                  w_gate=w_gate, b_gate=b_gate, w_out=w_out, mem_q_norm=mem_q_norm, mem_kv_norm=mem_kv_norm,
                  w_mem_q=w_mem_q, w_mem_kv=w_mem_kv, w_mem_o=w_mem_o)
    bsz, seq, _ = x_prompt.shape
    n_seq, n_new, _ = x_sample.shape
    depth = w_in.shape[0]
    past = page_table.shape[1] * cache_kv_latent.shape[2]
    tp, ts = bsz * seq, n_seq * n_new
    mem_tokens = mem_prompt.shape[1]
    assert seq % CHUNK == 0 and CHUNK % n_new == 0 and ts % CHUNK == 0 and n_new % 8 == 0

    tab_tile = _mixin_tile(tp, ts, seq)
    pos = jnp.concatenate([jnp.arange(seq, dtype=jnp.int32),
                           jnp.tile(past + jnp.arange(n_new, dtype=jnp.int32), tab_tile // n_new)])
    tabs = _rope_tables(pos)
    mask_p, mask_s = _spatial_masks(CHUNK), _spatial_masks(n_new)
    hist = CONV_K - 1

    mem2d = mem_prompt.reshape(bsz * mem_tokens, D_MODEL)
    cache_kr_t = jnp.swapaxes(cache_k_rope, 2, 3)
    mem_k_t = jnp.transpose(cache_mem_k, (0, 1, 3, 4, 2)).reshape(depth, n_seq, MEM_WIDTH, -1)
    mem_v_t = jnp.transpose(cache_mem_v, (0, 1, 3, 4, 2)).reshape(depth, n_seq, MEM_WIDTH, -1)
    w_fi, w_fo = w_ffn_in.astype(BF16), w_ffn_out.astype(BF16)
    outs = {k: [] for k in ("kvl_p", "kr_p", "ssm_p", "conv_p", "mk_p", "mv_p", "kvl_s", "kr_s", "ssm_s", "conv_s", "gv_s")}
    lws = [_layer_weights(l, params, n_new) for l in range(depth)]
    x = _ffn((x_prompt.reshape(tp, D_MODEL), x_sample.reshape(ts, D_MODEL)), ffn_norm[0, 0], w_fi, w_fo, 0, 0, tp)
    y_p = y_s = None
    for l in range(depth):
        lw = lws[l]
        q, k, ckv, ckvb, krp, misc, z, xbc, u, v = _mixin(x, lw, tabs, tp, seq)

        oh_p = _prompt_attention(q, k, ckvb, lw["w_uv"], bsz, seq)
        q_s = jnp.transpose(q[:, tp:, :].reshape(MLA_HEADS, n_seq, n_new, HEAD_PAD), (1, 0, 2, 3))
        oh_s = _sample_attention(q_s.reshape(n_seq, MLA_HEADS * n_new, HEAD_PAD), ckv, krp, lw["w_ukt_p"],
                                 lw["w_uv"], cache_kv_latent, cache_kr_t, page_table, l, tp, n_new)

        ys_p, st_p = _ssd_prompt(z, xbc, misc, lw, bsz, seq)
        sp = jnp.pad(state_conv[l], ((0, 0), (0, n_new - hist), (0, 0))).reshape(ts, CONV_DIM)
        h0 = state_ssm[l].reshape(n_seq, D_INNER, D_STATE)
        ys_s, st_s = _ssd_sample(z, xbc, misc, sp, h0, lw, tp, n_seq, n_new)

        (gm_p,) = _gmlp(u, v, lw["w_sp_p"], mask_p, lw["bias_p"], lw["gm_norm"], 0, tp, False)
        gm_s, vn_s = _gmlp(u, v, lw["w_sp_s"], mask_s, lw["bias_s"], lw["gm_norm"], tp, ts, True)

        x, qm = _merge(x, (oh_p, ys_p, gm_p), (oh_s, ys_s, gm_s), lw, tp)

        kv = _memory_kv(mem2d, lw["mem_kv_norm"], lw["w_mem_kv"])
        mk_p = kv[:, :MEM_WIDTH].reshape(bsz, mem_tokens, MEM_WIDTH)
        mv_p = kv[:, MEM_WIDTH:].reshape(bsz, mem_tokens, MEM_WIDTH)
        om_p = _mem_attention(qm, mk_p, mv_p, 0, bsz, seq)
        om_s = _mem_attention(qm, mem_k_t, mem_v_t, tp, n_seq, n_new, layer=l)
        om = (om_p, om_s)

        if l == depth - 1:
            y_p, y_s = _ffn(x, ffn_norm[l, 1], w_fi, w_fo, l, 1, tp, extra=om, w_extra=lw["w_mem_o"],
                            final_g=final_norm)
        else:
            x = _ffn(x, ffn_norm[l, 1], w_fi, w_fo, l, 1, tp, extra=om, w_extra=lw["w_mem_o"])
            x = _ffn(x, ffn_norm[l + 1, 0], w_fi, w_fo, l + 1, 0, tp)

        outs["kvl_p"].append(ckv[:tp].reshape(bsz, seq, KV_LORA))
        outs["kr_p"].append(krp[:tp, ROPE_LO:ROPE_LO + QK_ROPE].reshape(bsz, seq, QK_ROPE))
        outs["ssm_p"].append(st_p.reshape(bsz, SSM_HEADS, SSM_HEAD_DIM, D_STATE))
        outs["conv_p"].append(jnp.stack([xbc[(b + 1) * seq - hist:(b + 1) * seq] for b in range(bsz)]))
        outs["mk_p"].append(mk_p.reshape(bsz, mem_tokens, MEM_HEADS, MEM_HEAD_DIM))
        outs["mv_p"].append(mv_p.reshape(bsz, mem_tokens, MEM_HEADS, MEM_HEAD_DIM))
        outs["kvl_s"].append(ckv[tp:].reshape(n_seq, n_new, KV_LORA))
        outs["kr_s"].append(krp[tp:, ROPE_LO:ROPE_LO + QK_ROPE].reshape(n_seq, n_new, QK_ROPE))
        outs["ssm_s"].append(st_s.reshape(n_seq, SSM_HEADS, SSM_HEAD_DIM, D_STATE))
        outs["conv_s"].append(xbc[tp:].reshape(n_seq, n_new, CONV_DIM)[:, n_new - hist:])
        outs["gv_s"].append(vn_s.reshape(n_seq, n_new, GM_WIDTH))

    st = {k_: jnp.stack(v_) for k_, v_ in outs.items()}
    return (y_p.reshape(bsz, seq, D_MODEL), y_s.reshape(n_seq, n_new, D_MODEL),
            st["kvl_p"], st["kr_p"], st["ssm_p"], st["conv_p"], st["mk_p"], st["mv_p"],
            st["kvl_s"], st["kr_s"], st["ssm_s"], st["conv_s"], st["gv_s"])
```

```python
import functools
import math

import numpy as np
import jax
import jax.numpy as jnp
from jax import lax
from jax.experimental import pallas as pl
from jax.experimental.pallas import tpu as pltpu

F32 = jnp.float32
BF16 = jnp.bfloat16

D_MODEL = 1024
EPS = 1e-6
NEG_INF = -1e30
MLA_HEADS = 8
Q_LORA = 384
KV_LORA = 256
QK_NOPE = 64
QK_ROPE = 32
V_HEAD = 64
ROPE_THETA = 10000.0
MLA_SCALE = 1.0 / math.sqrt(QK_NOPE + QK_ROPE)
LOG2E = math.log2(math.e)
SSM_HEADS = 8
SSM_HEAD_DIM = 64
D_INNER = SSM_HEADS * SSM_HEAD_DIM
SSM_GROUPS = 2
D_STATE = 64
CONV_K = 4
CONV_DIM = D_INNER + 2 * SSM_GROUPS * D_STATE
GM_GROUPS = 8
GM_GROUP_DIM = 64
GM_WIDTH = GM_GROUPS * GM_GROUP_DIM
MEM_HEADS = 4
MEM_HEAD_DIM = 64
MEM_WIDTH = MEM_HEADS * MEM_HEAD_DIM
MEM_SCALE = 1.0 / math.sqrt(MEM_HEAD_DIM)
D_FF = 2816
IN_SIZES = (Q_LORA, KV_LORA, QK_ROPE, D_INNER, CONV_DIM, SSM_HEADS, GM_WIDTH, GM_WIDTH)

LANES = 128
CHUNK = 128
HEAD_PAD = LANES
ROPE_LO = QK_NOPE
ROPE_HALF = QK_ROPE // 2
MISC_DT = 96
VMEM_LIMIT = 56 * 1024 * 1024

P_CQ, P_CKV, P_Z, P_XBC, P_U, P_V, P_MISC, P_END = 0, 384, 640, 1152, 1920, 2432, 2944, 3072

NT_DIMS = (((1,), (1,)), ((), ()))
TN_DIMS = (((0,), (0,)), ((), ()))


def _pick(n, cands):
    for c in cands:
        if n % c == 0:
            return c
    raise ValueError(f"no tile in {cands} divides {n}")


def _rms(x, g):
    return x * lax.rsqrt(jnp.mean(x * x, axis=-1, keepdims=True) + EPS) * g


def _silu(x):
    return x * (1.0 / (1.0 + jnp.exp(-x)))


def _dot(a, b):
    return jnp.dot(a, b, preferred_element_type=F32)


def _dot_nt(a, b):
    return lax.dot_general(a, b, NT_DIMS, preferred_element_type=F32)


def _dot_exact(a, b, dims=None):
    if dims is None:
        return jnp.dot(a, b, precision=lax.Precision.HIGHEST, preferred_element_type=F32)
    return lax.dot_general(a, b, dims, precision=lax.Precision.HIGHEST, preferred_element_type=F32)


def _lane_tile(x, n):
    return x if n == 1 else jnp.concatenate([x] * n, axis=1)


def _const_spec(shape):
    nd = len(shape)
    return pl.BlockSpec(shape, lambda *_: (0,) * nd)


def _resident_spec(shape):
    nd = len(shape)
    return pl.BlockSpec(shape, lambda *_: (0,) * nd, pipeline_mode=pl.Buffered(1))


def _params(sem):
    return pltpu.CompilerParams(dimension_semantics=sem, vmem_limit_bytes=VMEM_LIMIT)


FF_SPLIT = 11


def _ffn_body(*refs, x_grouped, has_extra, has_final, npb):
    it = iter(refs)
    x_refs = (next(it), next(it)) if x_grouped else (next(it),) * 2
    e_refs = (None, None)
    if has_extra:
        e_refs = (next(it), next(it))
        we_ref = next(it)
    g_ref, win_ref, wout_ref = next(it), next(it), next(it)
    if has_final:
        fg_ref = next(it)
    o_refs = (next(it), next(it)) if has_final else (next(it),) * 2

    def compute(x_ref, e_ref, o_ref):
        x = x_ref[...]
        if has_extra:
            x = x + _dot(e_ref[...].astype(BF16), we_ref[...])
        h = _rms(x, g_ref[...]).astype(BF16)
        cw = D_FF // FF_SPLIT
        acc = jnp.zeros_like(x)
        for c in range(FF_SPLIT):
            gate = _dot(h, win_ref[:, c * cw:(c + 1) * cw])
            up = _dot(h, win_ref[:, D_FF + c * cw:D_FF + (c + 1) * cw])
            act = (_silu(gate) * up).astype(BF16)
            acc = acc + _dot(act, wout_ref[c * cw:(c + 1) * cw, :])
        out = x + 0.5 * acc
        o_ref[...] = _rms(out, fg_ref[...]) if has_final else out

    if x_grouped or has_extra or has_final:
        @pl.when(pl.program_id(0) < npb)
        def _():
            compute(x_refs[0], e_refs[0], o_refs[0])

        @pl.when(pl.program_id(0) >= npb)
        def _():
            compute(x_refs[1], e_refs[1], o_refs[1])
    else:
        compute(x_refs[0], None, o_refs[0])


def _ffn(x, norm_g, w_in_all, w_out_all, layer, which, tp, extra=None, w_extra=None, final_g=None):
    x_grouped = isinstance(x, tuple)
    t = x[0].shape[0] + x[1].shape[0] if x_grouped else x.shape[0]
    has_extra = extra is not None
    has_final = final_g is not None
    tm = _pick(math.gcd(tp, t - tp), (512, 256, 128, 64, 32, 16, 8))
    npb = tp // tm
    row = lambda i: (i, 0)
    prow, srow = _group_row_maps(npb)
    if x_grouped:
        args = [x[0], x[1]]
        specs = [pl.BlockSpec((tm, D_MODEL), prow), pl.BlockSpec((tm, D_MODEL), srow)]
    else:
        args, specs = [x], [pl.BlockSpec((tm, D_MODEL), row)]
    if has_extra:
        args += [extra[0], extra[1], w_extra]
        specs += [pl.BlockSpec((tm, extra[0].shape[1]), prow), pl.BlockSpec((tm, extra[1].shape[1]), srow),
                  _resident_spec(w_extra.shape)]
    pick = lambda i: (layer, which, 0, 0)
    args += [norm_g.reshape(1, D_MODEL), w_in_all, w_out_all]
    specs += [_const_spec((1, D_MODEL)),
              pl.BlockSpec((None, None) + w_in_all.shape[2:], pick, pipeline_mode=pl.Buffered(1)),
              pl.BlockSpec((None, None) + w_out_all.shape[2:], pick, pipeline_mode=pl.Buffered(1))]
    if has_final:
        args.append(final_g.reshape(1, D_MODEL))
        specs.append(_const_spec((1, D_MODEL)))
        out_shape = [jax.ShapeDtypeStruct((tp, D_MODEL), F32), jax.ShapeDtypeStruct((t - tp, D_MODEL), F32)]
        out_specs = [pl.BlockSpec((tm, D_MODEL), prow), pl.BlockSpec((tm, D_MODEL), srow)]
    else:
        out_shape = [jax.ShapeDtypeStruct((t, D_MODEL), F32)]
        out_specs = [pl.BlockSpec((tm, D_MODEL), row)]
    res = pl.pallas_call(
        functools.partial(_ffn_body, x_grouped=x_grouped, has_extra=has_extra, has_final=has_final, npb=npb),
        grid=(t // tm,), in_specs=specs, out_specs=out_specs, out_shape=out_shape,
        input_output_aliases={} if (x_grouped or has_final) else {0: 0},
        compiler_params=_params(("arbitrary",) if has_final else ("parallel",)), name="ffn",
    )(*args)
    return res if has_final else res[0]


def _rope(blk, c, s1, s2):
    return blk * c + pltpu.roll(blk, ROPE_HALF, 1) * s1 + pltpu.roll(blk, LANES - ROPE_HALF, 1) * s2


def _mixin_body(x_ref, g_ref, win_ref, qn_ref, wuq_ref, kvn_ref, wuk_ref, c_ref, s1_ref, s2_ref,
                q_ref, k_ref, ckv_ref, ckvb_ref, krp_ref, misc_ref, z_ref, xbc_ref, u_ref, v_ref):
    h = _rms(x_ref[...], g_ref[...]).astype(BF16)
    p = _dot(h, win_ref[...])
    z_ref[...] = p[:, P_Z:P_XBC]
    xbc_ref[...] = p[:, P_XBC:P_U]
    u_ref[...] = p[:, P_U:P_V]
    v_ref[...] = p[:, P_V:P_MISC]
    misc = p[:, P_MISC:P_END]
    misc_ref[...] = misc
    c, s1, s2 = c_ref[...], s1_ref[...], s2_ref[...]
    krp = _rope(misc, c, s1, s2)
    krp_ref[...] = krp
    ckvn = _rms(p[:, P_CKV:P_Z], kvn_ref[...])
    ckv_ref[...] = ckvn
    ckvb = ckvn.astype(BF16)
    ckvb_ref[...] = ckvb
    kn = _dot(ckvb, wuk_ref[...])
    qn = _rms(p[:, P_CQ:P_CKV], qn_ref[...]).astype(BF16)
    q = _dot(qn, wuq_ref[...]) * (MLA_SCALE * LOG2E)
    for hh in range(MLA_HEADS):
        sl = slice(hh * HEAD_PAD, (hh + 1) * HEAD_PAD)
        q_ref[hh] = _rope(q[:, sl], c, s1, s2).astype(BF16)
        k_ref[hh] = (kn[:, sl] + krp).astype(BF16)


def _mixin_tile(tp, ts, seq):
    return _pick(math.gcd(math.gcd(tp, ts), seq), (512, 256, 128, 64, 32, 16))


def _mixin(x, lw, tabs, tp, seq):
    t = x.shape[0]
    tm = _mixin_tile(tp, t - tp, seq)
    npb, nqt = tp // tm, seq // tm
    row = lambda i: (i, 0)
    trow = lambda i: (jnp.where(i < npb, i % nqt, nqt), 0)
    hrow = lambda i: (0, i, 0)
    wide = MLA_HEADS * HEAD_PAD
    outs = [
        ((MLA_HEADS, t, HEAD_PAD), BF16, pl.BlockSpec((MLA_HEADS, tm, HEAD_PAD), hrow)),
        ((MLA_HEADS, t, HEAD_PAD), BF16, pl.BlockSpec((MLA_HEADS, tm, HEAD_PAD), hrow)),
        ((t, KV_LORA), F32, pl.BlockSpec((tm, KV_LORA), row)),
        ((t, KV_LORA), BF16, pl.BlockSpec((tm, KV_LORA), row)),
        ((t, LANES), F32, pl.BlockSpec((tm, LANES), row)),
        ((t, LANES), F32, pl.BlockSpec((tm, LANES), row)),
        ((t, D_INNER), F32, pl.BlockSpec((tm, D_INNER), row)),
        ((t, CONV_DIM), F32, pl.BlockSpec((tm, CONV_DIM), row)),
        ((t, GM_WIDTH), F32, pl.BlockSpec((tm, GM_WIDTH), row)),
        ((t, GM_WIDTH), F32, pl.BlockSpec((tm, GM_WIDTH), row)),
    ]
    return pl.pallas_call(
        _mixin_body, grid=(t // tm,),
        in_specs=[pl.BlockSpec((tm, D_MODEL), row), _const_spec((1, D_MODEL)),
                  _resident_spec((D_MODEL, P_END)), _const_spec((1, Q_LORA)),
                  _resident_spec((Q_LORA, wide)), _const_spec((1, KV_LORA)),
                  _resident_spec((KV_LORA, wide)),
                  pl.BlockSpec((tm, LANES), trow), pl.BlockSpec((tm, LANES), trow),
                  pl.BlockSpec((tm, LANES), trow)],
        out_specs=[o[2] for o in outs],
        out_shape=[jax.ShapeDtypeStruct(o[0], o[1]) for o in outs],
        compiler_params=_params(("parallel",)), name="mixin",
    )(x, lw["mix_norm"], lw["w_in_p"], lw["q_norm"], lw["w_uq_p"], lw["kv_norm"], lw["w_uk_p"], *tabs)


def _pattn_body(qi_ref, kj_ref, q_ref, k_ref, v_ref, wuv_ref, o_ref, m_ref, l_ref, acc_ref, *, tq, kvb):
    n = pl.program_id(1)
    i = qi_ref[n]
    j = kj_ref[n]
    last = (i * tq + tq - 1) // kvb

    @pl.when(j == 0)
    def _():
        m_ref[...] = jnp.full(m_ref.shape, NEG_INF, F32)
        l_ref[...] = jnp.zeros(l_ref.shape, F32)
        acc_ref[...] = jnp.zeros(acc_ref.shape, F32)

    def step(masked):
        v = v_ref[...]
        if masked:
            qpos = i * tq + lax.broadcasted_iota(jnp.int32, (tq, kvb), 0)
            kpos = j * kvb + lax.broadcasted_iota(jnp.int32, (tq, kvb), 1)
            ok = kpos <= qpos
        for hh in range(MLA_HEADS):
            s = _dot_nt(q_ref[hh], k_ref[hh])
            if masked:
                s = jnp.where(ok, s, NEG_INF)
            m_prev = m_ref[hh]
            m_new = jnp.maximum(m_prev, jnp.max(s, axis=1, keepdims=True))
            alpha = jnp.exp2(m_prev - m_new)
            p = jnp.exp2(s - _lane_tile(m_new, kvb // LANES))
            psum = p[:, :LANES]
            for t in range(1, kvb // LANES):
                psum = psum + p[:, t * LANES:(t + 1) * LANES]
            l_ref[hh] = alpha * l_ref[hh] + psum
            acc_ref[hh] = acc_ref[hh] * _lane_tile(alpha, KV_LORA // LANES) + _dot(p.astype(BF16), v)
            m_ref[hh] = m_new

    @pl.when(j == last)
    def _():
        step(True)
        for hh in range(MLA_HEADS):
            o = (acc_ref[hh] / jnp.sum(l_ref[hh], axis=1, keepdims=True)).astype(BF16)
            o_ref[:, hh * V_HEAD:(hh + 1) * V_HEAD] = _dot(o, wuv_ref[hh]).astype(BF16)

    @pl.when(j != last)
    def _():
        step(False)


def _prompt_attention(q, k, vb, w_uv, bsz, seq):
    tq = _pick(seq, (512, 256, 128))
    kvb = _pick(seq, (512, 256, 128))
    nq, nkv = seq // tq, seq // kvb
    pairs = [(i, j) for i in range(nq) for j in range((i * tq + tq - 1) // kvb + 1)]
    qi = jnp.asarray(np.array([p[0] for p in pairs], np.int32))
    kj = jnp.asarray(np.array([p[1] for p in pairs], np.int32))
    grid_spec = pltpu.PrefetchScalarGridSpec(
        num_scalar_prefetch=2, grid=(bsz, len(pairs)),
        in_specs=[
            pl.BlockSpec((MLA_HEADS, tq, HEAD_PAD), lambda b, n, qi, kj: (0, b * nq + qi[n], 0)),
            pl.BlockSpec((MLA_HEADS, kvb, HEAD_PAD), lambda b, n, qi, kj: (0, b * nkv + kj[n], 0)),
            pl.BlockSpec((kvb, KV_LORA), lambda b, n, qi, kj: (b * nkv + kj[n], 0)),
            pl.BlockSpec((MLA_HEADS, KV_LORA, V_HEAD), lambda b, n, qi, kj: (0, 0, 0)),
        ],
        out_specs=pl.BlockSpec((tq, MLA_HEADS * V_HEAD), lambda b, n, qi, kj: (b * nq + qi[n], 0)),
        scratch_shapes=[pltpu.VMEM((MLA_HEADS, tq, LANES), F32), pltpu.VMEM((MLA_HEADS, tq, LANES), F32),
                        pltpu.VMEM((MLA_HEADS, tq, KV_LORA), F32)],
    )
    return pl.pallas_call(
        functools.partial(_pattn_body, tq=tq, kvb=kvb), grid_spec=grid_spec,
        out_shape=jax.ShapeDtypeStruct((bsz * seq, MLA_HEADS * V_HEAD), BF16),
        compiler_params=_params(("parallel", "arbitrary")), name="prompt_attn",
    )(qi, kj, q, k, vb, w_uv)


def _sattn_body(pt_ref, q_ref, ckvn_ref, krn_ref, wukt_ref, wuv_ref, ckv_hbm, ckr_hbm, o_ref,
                kbuf, rbuf, sem, qlat_ref, qrope_ref, m_ref, l_ref, acc_ref, *, layer, pc, n_seq, n_ch, page, nsub):
    b = pl.program_id(0)
    c = pl.program_id(1)
    step = b * n_ch + c
    slot = step % 2
    n_new = q_ref.shape[1] // MLA_HEADS

    def copies(bb, cc, sl):
        kv, kr = [], []
        for pg in range(pc):
            pid = pt_ref[bb, cc * pc + pg]
            kv.append(pltpu.make_async_copy(ckv_hbm.at[layer, pid], kbuf.at[sl, pl.ds(pg * page, page)], sem.at[0, sl]))
            kr.append(pltpu.make_async_copy(ckr_hbm.at[layer, pid], rbuf.at[sl, pg], sem.at[1, sl]))
        return kv + kr

    def start_all(cps):
        for t, cp in enumerate(cps):
            cp.start(priority=t % 2)

    @pl.when(step == 0)
    def _():
        start_all(copies(b, c, slot))

    @pl.when(step + 1 < n_seq * n_ch)
    def _():
        wrap = c + 1 == n_ch
        nb = jnp.where(wrap, b + 1, b)
        nc = jnp.where(wrap, 0, c + 1)
        start_all(copies(nb, nc, 1 - slot))

    @pl.when(c == 0)
    def _():
        m_ref[...] = jnp.full(m_ref.shape, NEG_INF, F32)
        l_ref[...] = jnp.zeros(l_ref.shape, F32)
        acc_ref[...] = jnp.zeros(acc_ref.shape, F32)
        q32 = q_ref[0].astype(F32)
        for hh in range(MLA_HEADS):
            qh = q32[hh * n_new:(hh + 1) * n_new, :].astype(BF16)
            qlat_ref[hh * n_new:(hh + 1) * n_new, :] = _dot(qh, wukt_ref[hh])
        qrope_ref[...] = q32[:, ROPE_LO:ROPE_LO + QK_ROPE]

    def merge(parts):
        m_prev = m_ref[...]
        m_new = m_prev
        for m, _, _ in parts:
            m_new = jnp.maximum(m_new, m)
        alpha = jnp.exp2(m_prev - m_new)
        l_new = alpha * l_ref[...]
        acc = acc_ref[...] * alpha
        for m, l, a in parts:
            w = jnp.exp2(m - m_new)
            l_new = l_new + w * l
            acc = acc + w * a
        m_ref[...] = m_new
        l_ref[...] = l_new
        acc_ref[...] = acc

    for cp in copies(b, c, slot):
        cp.wait()

    def chunk(last):
        qlat = qlat_ref[...].astype(BF16)
        qrope = qrope_ref[...].astype(BF16)
        pps = pc // nsub
        vals, scores = [], []
        for u in range(nsub):
            kc = kbuf[slot, u * pps * page:(u + 1) * pps * page, :].astype(BF16)
            rc = jnp.concatenate([rbuf[slot, u * pps + pg] for pg in range(pps)], axis=1).astype(BF16)
            vals.append(kc)
            scores.append(_dot_nt(qlat, kc) + _dot(qrope, rc))
        if last:
            pad = LANES - n_new
            kn = jnp.concatenate([ckvn_ref[...], jnp.zeros((pad, KV_LORA), F32)], axis=0).astype(BF16)
            rn = jnp.concatenate([krn_ref[...][:, ROPE_LO:ROPE_LO + QK_ROPE], jnp.zeros((pad, QK_ROPE), F32)],
                                 axis=0).astype(BF16)
            s = _dot_nt(qlat, kn) + _dot_nt(qrope, rn)
            tpos = lax.broadcasted_iota(jnp.int32, s.shape, 0) % n_new
            kpos = lax.broadcasted_iota(jnp.int32, s.shape, 1)
            vals.append(kn)
            scores.append(jnp.where(kpos <= tpos, s, NEG_INF))
        maxes = [jnp.max(s, axis=1, keepdims=True) for s in scores]
        probs = [jnp.exp2(s - m) for s, m in zip(scores, maxes)]
        sums = [jnp.sum(p, axis=1, keepdims=True) for p in probs]
        outs = [_dot(p.astype(BF16), v) for p, v in zip(probs, vals)]
        merge(list(zip(maxes, sums, outs)))
        if last:
            o = acc_ref[...] / l_ref[...]
            pieces = [_dot(o[hh * n_new:(hh + 1) * n_new, :].astype(BF16), wuv_ref[hh]) for hh in range(MLA_HEADS)]
            o_ref[0] = jnp.concatenate(pieces, axis=1).astype(BF16)

    @pl.when(c == n_ch - 1)
    def _():
        chunk(True)

    @pl.when(c != n_ch - 1)
    def _():
        chunk(False)


SATTN_PAGES = 128
SATTN_SPLIT = 2


def _sample_attention(q_s, ckv_all, krp_all, w_ukt_p, w_uv, cache_kv, cache_kr_t, page_table, layer, tp, n_new):
    n_seq, n_pages = page_table.shape
    page = cache_kv.shape[2]
    pc = _pick(n_pages, (SATTN_PAGES, 16, 8, 4, 2, 1))
    nsub = SATTN_SPLIT if pc % SATTN_SPLIT == 0 else 1
    n_ch = n_pages // pc
    rows = MLA_HEADS * n_new
    off = tp // n_new
    grid_spec = pltpu.PrefetchScalarGridSpec(
        num_scalar_prefetch=1, grid=(n_seq, n_ch),
        in_specs=[
            pl.BlockSpec((1, rows, HEAD_PAD), lambda b, c, pt: (b, 0, 0)),
            pl.BlockSpec((n_new, KV_LORA), lambda b, c, pt: (off + b, 0)),
            pl.BlockSpec((n_new, LANES), lambda b, c, pt: (off + b, 0)),
            pl.BlockSpec((MLA_HEADS, HEAD_PAD, KV_LORA), lambda b, c, pt: (0, 0, 0)),
            pl.BlockSpec((MLA_HEADS, KV_LORA, V_HEAD), lambda b, c, pt: (0, 0, 0)),
            pl.BlockSpec(memory_space=pl.ANY),
            pl.BlockSpec(memory_space=pl.ANY),
        ],
        out_specs=pl.BlockSpec((1, n_new, MLA_HEADS * V_HEAD), lambda b, c, pt: (b, 0, 0)),
        scratch_shapes=[
            pltpu.VMEM((2, pc * page, KV_LORA), F32), pltpu.VMEM((2, pc, QK_ROPE, page), F32),
            pltpu.SemaphoreType.DMA((2, 2)),
            pltpu.VMEM((rows, KV_LORA), F32), pltpu.VMEM((rows, QK_ROPE), F32),
            pltpu.VMEM((rows, 1), F32), pltpu.VMEM((rows, 1), F32), pltpu.VMEM((rows, KV_LORA), F32),
        ],
    )
    out = pl.pallas_call(
        functools.partial(_sattn_body, layer=layer, pc=pc, n_seq=n_seq, n_ch=n_ch, page=page, nsub=nsub),
        grid_spec=grid_spec,
        out_shape=jax.ShapeDtypeStruct((n_seq, n_new, MLA_HEADS * V_HEAD), BF16),
        compiler_params=_params(("arbitrary", "arbitrary")), name="sample_attn",
    )(page_table, q_s, ckv_all, krp_all, w_ukt_p, w_uv, cache_kv, cache_kr_t)
    return out.reshape(n_seq * n_new, MLA_HEADS * V_HEAD)


def _ssd_consts(seg):
    r = np.arange(CHUNK)
    same = (r[:, None] // seg) == (r[None, :] // seg)
    caus = (same & (r[None, :] <= r[:, None])).astype(np.float32)
    plast = (r[None, :] == (r[:, None] // seg) * seg + seg - 1).astype(np.float32)
    et = np.zeros((D_INNER, LANES), np.float32)
    et[np.arange(D_INNER), MISC_DT + np.arange(D_INNER) // SSM_HEAD_DIM] = 1.0
    return jnp.asarray(caus), jnp.asarray(plast), jnp.asarray(et)


def _ssd_tile(act, z, misc, caus, plast, et, dtb, alog, dskip, ng, h_in_fn, seg):
    nseg = CHUNK // seg
    lane = lax.broadcasted_iota(jnp.int32, (1, LANES), 1)
    hmask = (lane >= MISC_DT) & (lane < MISC_DT + SSM_HEADS)
    lo = lax.broadcasted_iota(jnp.int32, (CHUNK, LANES), 1) < SSM_HEAD_DIM
    xs = act[:, :D_INNER]
    bm = act[:, D_INNER:D_INNER + LANES]
    cm = act[:, D_INNER + LANES:D_INNER + 2 * LANES]
    dtv = jnp.where(hmask, jax.nn.softplus(misc + dtb), 0.0)
    a_row = jnp.where(hmask, -jnp.exp(alog), 0.0)
    acs = _dot_exact(caus, dtv * a_row)
    acs_last = _dot_exact(plast, acs)
    e_acs = jnp.exp(acs)
    to_end = jnp.exp(acs_last - acs)
    e_last = jnp.exp(acs_last)
    acs_t = acs.T
    dec_all = _dot_exact(et, e_last, NT_DIMS)

    bm_b = bm.astype(BF16)
    cm_g = [jnp.where(lo, cm, 0.0).astype(BF16), jnp.where(lo, 0.0, cm).astype(BF16)]
    g_mat = [_dot_nt(cm_g[g], bm_b) for g in range(SSM_GROUPS)]
    cm_half = [cm[:, :D_STATE].astype(BF16), cm[:, D_STATE:].astype(BF16)]

    def col(arr, hh):
        return arr[:, MISC_DT + hh:MISC_DT + hh + 1]

    def pair(arr, pb):
        return jnp.where(lo, col(arr, 2 * pb), col(arr, 2 * pb + 1))

    hpg = SSM_HEADS // SSM_GROUPS
    rows_g = hpg * SSM_HEAD_DIM
    y_off = []
    for g in range(SSM_GROUPS):
        parts = []
        for j in range(nseg):
            hj = h_in_fn(j)[g * rows_g:(g + 1) * rows_g, :].astype(BF16)
            parts.append(_dot_nt(cm_half[g][j * seg:(j + 1) * seg, :], hj))
        y_off.append(parts[0] if nseg == 1 else jnp.concatenate(parts, axis=0))

    y_blocks, xte_blocks = [], []
    for pb in range(SSM_HEADS // 2):
        g = (2 * pb) // hpg
        sl = slice(pb * LANES, (pb + 1) * LANES)
        xs_p = xs[:, sl]
        xdt = xs_p * pair(dtv, pb)
        xte_blocks.append((xdt * pair(to_end, pb)).astype(BF16))
        xdt_b = xdt.astype(BF16)
        yd = []
        for hh in (2 * pb, 2 * pb + 1):
            segm = col(acs, hh) - acs_t[MISC_DT + hh:MISC_DT + hh + 1, :]
            decay = jnp.exp(jnp.where(caus > 0.0, segm, NEG_INF))
            yd.append(_dot((g_mat[g] * decay).astype(BF16), xdt_b))
        y_d = jnp.where(lo, yd[0], yd[1])
        off_sl = slice((pb % (hpg // 2)) * LANES, (pb % (hpg // 2) + 1) * LANES)
        y_o = y_off[g][:, off_sl] * pair(e_acs, pb)
        y_blocks.append(y_d + y_o + dskip[:, sl] * xs_p)
    y = jnp.concatenate(y_blocks, axis=1)
    yn = _rms(y * _silu(z), ng)

    xte = jnp.concatenate(xte_blocks, axis=1)
    rowi = lax.broadcasted_iota(jnp.int32, (D_INNER, D_STATE), 0) < rows_g
    new_states = []
    if nseg == 1:
        s_all = lax.dot_general(xte, bm_b, TN_DIMS, preferred_element_type=F32)
        s_sel = jnp.where(rowi, s_all[:, :D_STATE], s_all[:, D_STATE:])
        new_states.append(h_in_fn(0) * dec_all[:, 0:1] + s_sel)
    else:
        tr = lax.broadcasted_iota(jnp.int32, (CHUNK, LANES), 0) // seg
        bbd = jnp.concatenate([jnp.where(tr == j, bm, 0.0) for j in range(nseg)], axis=1).astype(BF16)
        s_all = lax.dot_general(xte, bbd, TN_DIMS, preferred_element_type=F32)
        for j in range(nseg):
            sj = s_all[:, j * LANES:(j + 1) * LANES]
            s_sel = jnp.where(rowi, sj[:, :D_STATE], sj[:, D_STATE:])
            new_states.append(h_in_fn(j) * dec_all[:, j * seg:j * seg + 1] + s_sel)
    return yn, new_states


def _ssd_prompt_body(*refs, bsz):
    zs, xbcs, miscs = refs[:bsz], refs[bsz:2 * bsz], refs[2 * bsz:3 * bsz]
    (caus_ref, plast_ref, et_ref, cw_ref, cb_ref, dtb_ref, alog_ref, dskip_ref, ng_ref,
     y_ref, st_ref, xpad_ref) = refs[3 * bsz:]
    c = pl.program_id(0)
    hist = CONV_K - 1

    @pl.when(c == 0)
    def _():
        xpad_ref[:, 0:8, :] = jnp.zeros((bsz, 8, CONV_DIM), F32)
        st_ref[...] = jnp.zeros(st_ref.shape, F32)

    for b in range(bsz):
        x = xbcs[b][...]
        xpad_ref[b, 8:8 + CHUNK, :] = x
        conv = cb_ref[...] + x * cw_ref[hist:hist + 1, :]
        for k in range(hist):
            conv = conv + xpad_ref[b, 8 - hist + k:8 - hist + k + CHUNK, :] * cw_ref[k:k + 1, :]
        xpad_ref[b, 0:8, :] = x[CHUNK - 8:, :]
        act = _silu(conv)
        yn, new = _ssd_tile(act, zs[b][...], miscs[b][...], caus_ref[...], plast_ref[...], et_ref[...],
                            dtb_ref[...], alog_ref[...], dskip_ref[...], ng_ref[...], lambda j, b=b: st_ref[b], CHUNK)
        y_ref[b] = yn.astype(BF16)
        st_ref[b] = new[0]


def _ssd_sample_body(z_ref, xbc_ref, misc_ref, sp_ref, h0_ref, caus_ref, plast_ref, et_ref, cw_ref, cb_ref,
                     dtb_ref, alog_ref, dskip_ref, ng_ref, y_ref, st_ref, *, seg):
    hist = CONV_K - 1
    x = xbc_ref[...]
    sp = sp_ref[...]
    tpos = lax.broadcasted_iota(jnp.int32, (CHUNK, CONV_DIM), 0) % seg
    conv = cb_ref[...] + x * cw_ref[hist:hist + 1, :]
    for j in range(1, CONV_K):
        xin = pltpu.roll(x, j, 0)
        up = hist - j
        sin = sp if up == 0 else pltpu.roll(sp, CHUNK - up, 0)
        conv = conv + jnp.where(tpos >= j, xin, sin) * cw_ref[hist - j:hist - j + 1, :]
    act = _silu(conv)
    yn, new = _ssd_tile(act, z_ref[...], misc_ref[...], caus_ref[...], plast_ref[...], et_ref[...],
                        dtb_ref[...], alog_ref[...], dskip_ref[...], ng_ref[...], lambda j: h0_ref[j], seg)
    y_ref[...] = yn.astype(BF16)
    for j in range(len(new)):
        st_ref[j] = new[j]


def _ssd_param_args(lw):
    return [lw["conv_w"], lw["conv_b"], lw["dtb_row"], lw["alog_row"], lw["dskip_exp"], lw["ssm_norm"]]


def _ssd_param_specs():
    return [_const_spec((CONV_K, CONV_DIM)), _const_spec((1, CONV_DIM)), _const_spec((1, LANES)),
            _const_spec((1, LANES)), _const_spec((1, D_INNER)), _const_spec((1, D_INNER))]


def _ssd_prompt(z, xbc, misc, lw, bsz, seq):
    nc = seq // CHUNK
    caus, plast, et = _ssd_consts(CHUNK)
    rows = [(lambda c, b=b: (b * nc + c, 0)) for b in range(bsz)]
    y, st = pl.pallas_call(
        functools.partial(_ssd_prompt_body, bsz=bsz), grid=(nc,),
        in_specs=[pl.BlockSpec((CHUNK, D_INNER), r) for r in rows]
        + [pl.BlockSpec((CHUNK, CONV_DIM), r) for r in rows] + [pl.BlockSpec((CHUNK, LANES), r) for r in rows]
        + [_const_spec((CHUNK, CHUNK)), _const_spec((CHUNK, CHUNK)), _const_spec((D_INNER, LANES))]
        + _ssd_param_specs(),
        out_specs=[pl.BlockSpec((bsz, CHUNK, D_INNER), lambda c: (0, c, 0)),
                   pl.BlockSpec((bsz, D_INNER, D_STATE), lambda c: (0, 0, 0))],
        out_shape=[jax.ShapeDtypeStruct((bsz, seq, D_INNER), BF16),
                   jax.ShapeDtypeStruct((bsz, D_INNER, D_STATE), F32)],
        scratch_shapes=[pltpu.VMEM((bsz, 8 + CHUNK, CONV_DIM), F32)],
        compiler_params=_params(("arbitrary",)), name="ssd_prompt",
    )(*([z] * bsz), *([xbc] * bsz), *([misc] * bsz), caus, plast, et, *_ssd_param_args(lw))
    return y.reshape(bsz * seq, D_INNER), st


def _ssd_sample(z, xbc, misc, sp, h0, lw, tp, n_seq, seg):
    nseg = CHUNK // seg
    nt = n_seq // nseg
    off = tp // CHUNK
    caus, plast, et = _ssd_consts(seg)
    row = lambda i: (off + i, 0)
    y, st = pl.pallas_call(
        functools.partial(_ssd_sample_body, seg=seg), grid=(nt,),
        in_specs=[pl.BlockSpec((CHUNK, D_INNER), row), pl.BlockSpec((CHUNK, CONV_DIM), row),
                  pl.BlockSpec((CHUNK, LANES), row), pl.BlockSpec((CHUNK, CONV_DIM), lambda i: (i, 0)),
                  pl.BlockSpec((nseg, D_INNER, D_STATE), lambda i: (i, 0, 0)),
                  _const_spec((CHUNK, CHUNK)), _const_spec((CHUNK, CHUNK)),
                  _const_spec((D_INNER, LANES))] + _ssd_param_specs(),
        out_specs=[pl.BlockSpec((CHUNK, D_INNER), lambda i: (i, 0)),
                   pl.BlockSpec((nseg, D_INNER, D_STATE), lambda i: (i, 0, 0))],
        out_shape=[jax.ShapeDtypeStruct((n_seq * seg, D_INNER), BF16),
                   jax.ShapeDtypeStruct((n_seq, D_INNER, D_STATE), F32)],
        compiler_params=_params(("parallel",)), name="ssd_sample",
    )(z, xbc, misc, sp, h0, caus, plast, et, *_ssd_param_args(lw))
    return y, st


def _gmlp_body(u_ref, v_ref, w_ref, mask_ref, b_ref, ng_ref, o_ref, *vn_out, reps):
    lo = lax.broadcasted_iota(jnp.int32, (CHUNK, LANES), 1) < GM_GROUP_DIM
    mask = mask_ref[...] > 0.0
    wm = [jnp.where(mask, w_ref[g], 0.0).astype(BF16) for g in range(GM_GROUPS)]
    bias = b_ref[...]
    for r in range(reps):
        rs = slice(r * CHUNK, (r + 1) * CHUNK)
        ug = jax.nn.gelu(u_ref[rs, :])
        vn = _rms(jax.nn.gelu(v_ref[rs, :]), ng_ref[...])
        if vn_out:
            vn_out[0][rs, :] = vn
        vb = vn.astype(BF16)
        for pb in range(GM_GROUPS // 2):
            sl = slice(pb * LANES, (pb + 1) * LANES)
            s = jnp.where(lo, _dot(wm[2 * pb], vb[:, sl]), _dot(wm[2 * pb + 1], vb[:, sl])) + bias[:, sl]
            o_ref[rs, sl] = (ug[:, sl] * s).astype(BF16)


def _gmlp(u, v, w_sp, mask, bias_exp, norm_g, row_off, n_rows, want_vn):
    reps = _pick(n_rows // CHUNK, (4, 2, 1))
    tm = reps * CHUNK
    off = row_off // tm
    row = lambda i: (off + i, 0)
    out_shape = [jax.ShapeDtypeStruct((n_rows, GM_WIDTH), BF16)]
    out_specs = [pl.BlockSpec((tm, GM_WIDTH), lambda i: (i, 0))]
    if want_vn:
        out_shape.append(jax.ShapeDtypeStruct((n_rows, GM_WIDTH), F32))
        out_specs.append(pl.BlockSpec((tm, GM_WIDTH), lambda i: (i, 0)))
    return pl.pallas_call(
        functools.partial(_gmlp_body, reps=reps), grid=(n_rows // tm,),
        in_specs=[pl.BlockSpec((tm, GM_WIDTH), row), pl.BlockSpec((tm, GM_WIDTH), row),
                  _const_spec((GM_GROUPS, CHUNK, CHUNK)), _const_spec((CHUNK, CHUNK)),
                  _const_spec((CHUNK, GM_WIDTH)), _const_spec((1, GM_WIDTH))],
        out_specs=out_specs, out_shape=out_shape,
        compiler_params=_params(("parallel",)), name="gmlp",
    )(u, v, w_sp, mask, bias_exp, norm_g)


def _merge_body(x_ref, ohp_ref, ysp_ref, gmp_ref, ohs_ref, yss_ref, gms_ref, ng_ref, wg_ref, bg_ref, womla_ref,
                wossm_ref, wogm_ref, wout_ref, mqn_ref, wmq_ref, xo_ref, qm_ref, *, npb):
    def compute(oh_ref, ys_ref, gm_ref):
        x = x_ref[...]
        h = _rms(x, ng_ref[...]).astype(BF16)
        mix = None
        for br, (b_ref, w_ref) in enumerate(((oh_ref, womla_ref), (ys_ref, wossm_ref), (gm_ref, wogm_ref))):
            sl = slice(br * D_MODEL, (br + 1) * D_MODEL)
            gate = 1.0 / (1.0 + jnp.exp(-(_dot(h, wg_ref[:, sl]) + bg_ref[:, sl])))
            term = gate * _dot(b_ref[...], w_ref[...])
            mix = term if mix is None else mix + term
        x2 = x + _dot(mix.astype(BF16), wout_ref[...])
        xo_ref[...] = x2
        hq = _rms(x2, mqn_ref[...]).astype(BF16)
        qm_ref[...] = _dot(hq, wmq_ref[...])

    @pl.when(pl.program_id(0) < npb)
    def _():
        compute(ohp_ref, ysp_ref, gmp_ref)

    @pl.when(pl.program_id(0) >= npb)
    def _():
        compute(ohs_ref, yss_ref, gms_ref)


def _group_row_maps(npb):
    return (lambda i: (jnp.minimum(i, npb - 1), 0)), (lambda i: (jnp.maximum(i - npb, 0), 0))


def _merge(x, branches_p, branches_s, lw, tp):
    t = x.shape[0]
    tm = _pick(math.gcd(tp, t - tp), (512, 256, 128, 64, 32, 16))
    row = lambda i: (i, 0)
    prow, srow = _group_row_maps(tp // tm)
    half = MLA_HEADS * V_HEAD
    widths = (half, D_INNER, GM_WIDTH)
    return pl.pallas_call(
        functools.partial(_merge_body, npb=tp // tm), grid=(t // tm,),
        in_specs=[pl.BlockSpec((tm, D_MODEL), row)]
        + [pl.BlockSpec((tm, w), prow) for w in widths] + [pl.BlockSpec((tm, w), srow) for w in widths]
        + [_const_spec((1, D_MODEL)), _resident_spec((D_MODEL, 3 * D_MODEL)), _const_spec((1, 3 * D_MODEL)),
           _resident_spec((half, D_MODEL)), _resident_spec((D_INNER, D_MODEL)),
           _resident_spec((GM_WIDTH, D_MODEL)), _resident_spec((D_MODEL, D_MODEL)),
           _const_spec((1, D_MODEL)), _resident_spec((D_MODEL, MEM_WIDTH))],
        out_specs=[pl.BlockSpec((tm, D_MODEL), row), pl.BlockSpec((tm, MEM_WIDTH), row)],
        out_shape=[jax.ShapeDtypeStruct((t, D_MODEL), F32), jax.ShapeDtypeStruct((t, MEM_WIDTH), F32)],
        input_output_aliases={0: 0}, compiler_params=_params(("parallel",)), name="merge",
    )(x, *branches_p, *branches_s, lw["mix_norm"], lw["w_gate"], lw["b_gate"], lw["w_o_mla"], lw["w_o_ssm"],
      lw["w_o_gm"], lw["w_out"], lw["mem_q_norm"], lw["w_mem_q"])


def _memattn_body(q_ref, k_ref, v_ref, o_ref, *, sb, feat_major):
    q = q_ref[...]
    rows = q.shape[0] // sb
    if feat_major:
        m = k_ref.shape[2]
        kb = jnp.concatenate([k_ref[e] for e in range(sb)], axis=1).astype(BF16)
        vb = jnp.concatenate([v_ref[e] for e in range(sb)], axis=1).astype(BF16)
    else:
        m = k_ref.shape[1]
        kb = k_ref[...].reshape(sb * m, MEM_WIDTH).astype(BF16)
        vb = v_ref[...].reshape(sb * m, MEM_WIDTH).astype(BF16)
    lane = lax.broadcasted_iota(jnp.int32, q.shape, 1) // MEM_HEAD_DIM
    if sb > 1:
        own = (lax.broadcasted_iota(jnp.int32, (q.shape[0], sb * m), 0) // rows
               == lax.broadcasted_iota(jnp.int32, (q.shape[0], sb * m), 1) // m)
    out = jnp.zeros(q.shape, F32)
    for hh in range(MEM_HEADS):
        sel = lane == hh
        qh = jnp.where(sel, q, 0.0).astype(BF16)
        s = (_dot(qh, kb) if feat_major else _dot_nt(qh, kb)) * MEM_SCALE
        if sb > 1:
            s = jnp.where(own, s, NEG_INF)
        p = jnp.exp(s - jnp.max(s, axis=1, keepdims=True))
        p = (p / jnp.sum(p, axis=1, keepdims=True)).astype(BF16)
        out = out + jnp.where(sel, _dot_nt(p, vb) if feat_major else _dot(p, vb), 0.0)
    o_ref[...] = out


MEMATTN_BATCH = 8


def _mem_attention(qm, mem_k, mem_v, row_off, nb, rows_per_b, layer=None):
    tm = _pick(rows_per_b, (512, 256, 128, 64, 32, 16, 8))
    nt = rows_per_b // tm
    sb = MEMATTN_BATCH if (nt == 1 and nb % MEMATTN_BATCH == 0 and tm * MEMATTN_BATCH <= 512) else 1
    off = row_off // (tm * sb)
    if layer is None:
        kv_spec = pl.BlockSpec((sb,) + mem_k.shape[1:], lambda b, i: (b, 0, 0))
    else:
        kv_spec = pl.BlockSpec((None, sb) + mem_k.shape[2:], lambda b, i: (layer, b, 0, 0))
    return pl.pallas_call(
        functools.partial(_memattn_body, sb=sb, feat_major=layer is not None), grid=(nb // sb, nt),
        in_specs=[pl.BlockSpec((tm * sb, MEM_WIDTH), lambda b, i: (off + b * nt + i, 0)), kv_spec, kv_spec],
        out_specs=pl.BlockSpec((tm * sb, MEM_WIDTH), lambda b, i: (b * nt + i, 0)),
        out_shape=jax.ShapeDtypeStruct((nb * rows_per_b, MEM_WIDTH), F32),
        compiler_params=_params(("parallel", "parallel")), name="mem_attn",
    )(qm, mem_k, mem_v)


def _memkv_body(m_ref, g_ref, w_ref, o_ref):
    o_ref[...] = _dot(_rms(m_ref[...], g_ref[...]).astype(BF16), w_ref[...])


def _memory_kv(mem2d, norm_g, w_kv):
    t = mem2d.shape[0]
    tm = _pick(t, (256, 128, 64, 32, 16, 8))
    return pl.pallas_call(
        _memkv_body, grid=(t // tm,),
        in_specs=[pl.BlockSpec((tm, D_MODEL), lambda i: (i, 0)), _const_spec((1, D_MODEL)),
                  _const_spec((D_MODEL, 2 * MEM_WIDTH))],
        out_specs=pl.BlockSpec((tm, 2 * MEM_WIDTH), lambda i: (i, 0)),
        out_shape=jax.ShapeDtypeStruct((t, 2 * MEM_WIDTH), F32),
        compiler_params=_params(("parallel",)), name="mem_kv",
    )(mem2d, norm_g, w_kv)


def _layer_weights(l, p, seg):
    w_in = p["w_in"][l]
    offs = np.cumsum((0,) + IN_SIZES)
    cq, ckv, kr, z, xbc, dt, u, v = [w_in[:, offs[i]:offs[i + 1]] for i in range(len(IN_SIZES))]
    zc = lambda n: jnp.zeros((D_MODEL, n), F32)
    w_in_p = jnp.concatenate(
        [cq, ckv, z, xbc, u, v, zc(ROPE_LO), kr, dt, zc(LANES - ROPE_LO - QK_ROPE - SSM_HEADS)], axis=1)
    hd = QK_NOPE + QK_ROPE
    w_uq_p = jnp.pad(p["w_uq"][l].reshape(Q_LORA, MLA_HEADS, hd), ((0, 0), (0, 0), (0, HEAD_PAD - hd)))
    w_uk = p["w_uk"][l]
    w_uk_p = jnp.pad(jnp.transpose(w_uk, (1, 0, 2)), ((0, 0), (0, 0), (0, HEAD_PAD - QK_NOPE)))
    w_ukt_p = jnp.pad(jnp.transpose(w_uk, (0, 2, 1)), ((0, 0), (0, HEAD_PAD - QK_NOPE), (0, 0)))
    lane_pad = lambda a: jnp.pad(a.reshape(1, SSM_HEADS), ((0, 0), (MISC_DT, LANES - MISC_DT - SSM_HEADS)))
    w_sp = p["w_spatial"][l]
    b_sp = p["b_spatial"][l]
    bias_p = jnp.repeat(b_sp.T, GM_GROUP_DIM, axis=1)
    reps = CHUNK // seg
    w_sp_s = jnp.tile(w_sp[:, :seg, :seg], (1, reps, reps))
    bias_s = jnp.tile(jnp.repeat(b_sp[:, :seg].T, GM_GROUP_DIM, axis=1), (reps, 1))
    row = lambda a: a.reshape(1, -1)
    return {
        "mix_norm": row(p["mix_norm"][l]), "w_in_p": w_in_p.astype(BF16),
        "q_norm": row(p["q_norm"][l]), "w_uq_p": w_uq_p.reshape(Q_LORA, MLA_HEADS * HEAD_PAD).astype(BF16),
        "kv_norm": row(p["kv_norm"][l]), "w_uk_p": w_uk_p.reshape(KV_LORA, MLA_HEADS * HEAD_PAD).astype(BF16),
        "w_ukt_p": w_ukt_p.astype(BF16), "w_uv": p["w_uv"][l].astype(BF16),
        "w_o_mla": p["w_o_mla"][l].astype(BF16),
        "conv_w": p["conv_w"][l], "conv_b": row(p["conv_b"][l]),
        "dtb_row": lane_pad(p["dt_bias"][l]), "alog_row": lane_pad(p["a_log"][l]),
        "dskip_exp": row(jnp.repeat(p["d_skip"][l], SSM_HEAD_DIM)), "ssm_norm": row(p["ssm_norm"][l]),
        "w_o_ssm": p["w_o_ssm"][l].astype(BF16),
        "gm_norm": row(p["gm_norm"][l]), "w_sp_p": w_sp, "bias_p": bias_p, "w_sp_s": w_sp_s, "bias_s": bias_s,
        "w_o_gm": p["w_o_gm"][l].astype(BF16),
        "w_gate": p["w_gate"][l].astype(BF16), "b_gate": row(p["b_gate"][l]), "w_out": p["w_out"][l].astype(BF16),
        "mem_q_norm": row(p["mem_q_norm"][l]), "mem_kv_norm": row(p["mem_kv_norm"][l]),
        "w_mem_q": p["w_mem_q"][l].astype(BF16), "w_mem_kv": p["w_mem_kv"][l].astype(BF16),
        "w_mem_o": p["w_mem_o"][l].astype(BF16),
    }


def _rope_tables(pos):
    inv = ROPE_THETA ** (-jnp.arange(ROPE_HALF, dtype=F32) / ROPE_HALF)
    ang = pos.astype(F32)[:, None] * inv[None, :]
    cos, sin = jnp.cos(ang), jnp.sin(ang)
    n = pos.shape[0]
    zero = lambda w: jnp.zeros((n, w), F32)
    tail = LANES - ROPE_LO - QK_ROPE
    c = jnp.concatenate([jnp.ones((n, ROPE_LO), F32), cos, cos, zero(tail)], axis=1)
    s1 = jnp.concatenate([zero(ROPE_LO + ROPE_HALF), sin, zero(tail)], axis=1)
    s2 = jnp.concatenate([zero(ROPE_LO), -sin, zero(ROPE_HALF + tail)], axis=1)
    return c, s1, s2


def _spatial_masks(seg):
    r = np.arange(CHUNK)
    same = (r[:, None] // seg) == (r[None, :] // seg)
    return jnp.asarray((same & (r[None, :] <= r[:, None])).astype(np.float32))


def kernel(x_prompt, x_sample, mem_prompt, cache_kv_latent, cache_k_rope, state_ssm, state_conv, cache_mem_k, cache_mem_v, page_table, ffn_norm, w_ffn_in, w_ffn_out, mix_norm, w_in, q_norm, w_uq, kv_norm, w_uk, w_uv, w_o_mla, conv_w, conv_b, dt_bias, a_log, d_skip, ssm_norm, w_o_ssm, gm_norm, w_spatial, b_spatial, w_o_gm, w_gate, b_gate, w_out, mem_q_norm, mem_kv_norm, w_mem_q, w_mem_kv, w_mem_o, final_norm):
    params = dict(ffn_norm=ffn_norm, w_ffn_in=w_ffn_in, w_ffn_out=w_ffn_out, mix_norm=mix_norm, w_in=w_in,
                  q_norm=q_norm, w_uq=w_uq, kv_norm=kv_norm, w_uk=w_uk, w_uv=w_uv, w_o_mla=w_o_mla,
                  conv_w=conv_w, conv_b=conv_b, dt_bias=dt_bias, a_log=a_log, d_skip=d_skip, ssm_norm=ssm_norm,
                  w_o_ssm=w_o_ssm, gm_norm=gm_norm, w_spatial=w_spatial, b_spatial=b_spatial, w_o_gm=w_o_gm,
                  w_gate=w_gate, b_gate=b_gate, w_out=w_out, mem_q_norm=mem_q_norm, mem_kv_norm=mem_kv_norm,
                  w_mem_q=w_mem_q, w_mem_kv=w_mem_kv, w_mem_o=w_mem_o)
    bsz, seq, _ = x_prompt.shape
    n_seq, n_new, _ = x_sample.shape
    depth = w_in.shape[0]
    past = page_table.shape[1] * cache_kv_latent.shape[2]
    tp, ts = bsz * seq, n_seq * n_new
    mem_tokens = mem_prompt.shape[1]
    assert seq % CHUNK == 0 and CHUNK % n_new == 0 and ts % CHUNK == 0 and n_new % 8 == 0

    tab_tile = _mixin_tile(tp, ts, seq)
    pos = jnp.concatenate([jnp.arange(seq, dtype=jnp.int32),
                           jnp.tile(past + jnp.arange(n_new, dtype=jnp.int32), tab_tile // n_new)])
    tabs = _rope_tables(pos)
    mask_p, mask_s = _spatial_masks(CHUNK), _spatial_masks(n_new)
    hist = CONV_K - 1

    mem2d = mem_prompt.reshape(bsz * mem_tokens, D_MODEL)
    cache_kr_t = jnp.swapaxes(cache_k_rope, 2, 3)
    mem_k_t = jnp.transpose(cache_mem_k, (0, 1, 3, 4, 2)).reshape(depth, n_seq, MEM_WIDTH, -1)
    mem_v_t = jnp.transpose(cache_mem_v, (0, 1, 3, 4, 2)).reshape(depth, n_seq, MEM_WIDTH, -1)
    w_fi, w_fo = w_ffn_in.astype(BF16), w_ffn_out.astype(BF16)
    outs = {k: [] for k in ("kvl_p", "kr_p", "ssm_p", "conv_p", "mk_p", "mv_p", "kvl_s", "kr_s", "ssm_s", "conv_s", "gv_s")}
    lws = [_layer_weights(l, params, n_new) for l in range(depth)]
    x = _ffn((x_prompt.reshape(tp, D_MODEL), x_sample.reshape(ts, D_MODEL)), ffn_norm[0, 0], w_fi, w_fo, 0, 0, tp)
    y_p = y_s = None
    for l in range(depth):
        lw = lws[l]
        q, k, ckv, ckvb, krp, misc, z, xbc, u, v = _mixin(x, lw, tabs, tp, seq)

        oh_p = _prompt_attention(q, k, ckvb, lw["w_uv"], bsz, seq)
        q_s = jnp.transpose(q[:, tp:, :].reshape(MLA_HEADS, n_seq, n_new, HEAD_PAD), (1, 0, 2, 3))
        oh_s = _sample_attention(q_s.reshape(n_seq, MLA_HEADS * n_new, HEAD_PAD), ckv, krp, lw["w_ukt_p"],
                                 lw["w_uv"], cache_kv_latent, cache_kr_t, page_table, l, tp, n_new)

        ys_p, st_p = _ssd_prompt(z, xbc, misc, lw, bsz, seq)
        sp = jnp.pad(state_conv[l], ((0, 0), (0, n_new - hist), (0, 0))).reshape(ts, CONV_DIM)
        h0 = state_ssm[l].reshape(n_seq, D_INNER, D_STATE)
        ys_s, st_s = _ssd_sample(z, xbc, misc, sp, h0, lw, tp, n_seq, n_new)

        (gm_p,) = _gmlp(u, v, lw["w_sp_p"], mask_p, lw["bias_p"], lw["gm_norm"], 0, tp, False)
        gm_s, vn_s = _gmlp(u, v, lw["w_sp_s"], mask_s, lw["bias_s"], lw["gm_norm"], tp, ts, True)

        x, qm = _merge(x, (oh_p, ys_p, gm_p), (oh_s, ys_s, gm_s), lw, tp)

        kv = _memory_kv(mem2d, lw["mem_kv_norm"], lw["w_mem_kv"])
        mk_p = kv[:, :MEM_WIDTH].reshape(bsz, mem_tokens, MEM_WIDTH)
        mv_p = kv[:, MEM_WIDTH:].reshape(bsz, mem_tokens, MEM_WIDTH)
        om_p = _mem_attention(qm, mk_p, mv_p, 0, bsz, seq)
        om_s = _mem_attention(qm, mem_k_t, mem_v_t, tp, n_seq, n_new, layer=l)
        om = (om_p, om_s)

        if l == depth - 1:
            y_p, y_s = _ffn(x, ffn_norm[l, 1], w_fi, w_fo, l, 1, tp, extra=om, w_extra=lw["w_mem_o"],
                            final_g=final_norm)
        else:
            x = _ffn(x, ffn_norm[l, 1], w_fi, w_fo, l, 1, tp, extra=om, w_extra=lw["w_mem_o"])
            x = _ffn(x, ffn_norm[l + 1, 0], w_fi, w_fo, l + 1, 0, tp)

        outs["kvl_p"].append(ckv[:tp].reshape(bsz, seq, KV_LORA))
        outs["kr_p"].append(krp[:tp, ROPE_LO:ROPE_LO + QK_ROPE].reshape(bsz, seq, QK_ROPE))
        outs["ssm_p"].append(st_p.reshape(bsz, SSM_HEADS, SSM_HEAD_DIM, D_STATE))
        outs["conv_p"].append(jnp.stack([xbc[(b + 1) * seq - hist:(b + 1) * seq] for b in range(bsz)]))
        outs["mk_p"].append(mk_p.reshape(bsz, mem_tokens, MEM_HEADS, MEM_HEAD_DIM))
        outs["mv_p"].append(mv_p.reshape(bsz, mem_tokens, MEM_HEADS, MEM_HEAD_DIM))
        outs["kvl_s"].append(ckv[tp:].reshape(n_seq, n_new, KV_LORA))
        outs["kr_s"].append(krp[tp:, ROPE_LO:ROPE_LO + QK_ROPE].reshape(n_seq, n_new, QK_ROPE))
        outs["ssm_s"].append(st_s.reshape(n_seq, SSM_HEADS, SSM_HEAD_DIM, D_STATE))
        outs["conv_s"].append(xbc[tp:].reshape(n_seq, n_new, CONV_DIM)[:, n_new - hist:])
        outs["gv_s"].append(vn_s.reshape(n_seq, n_new, GM_WIDTH))

    st = {k_: jnp.stack(v_) for k_, v_ in outs.items()}
    return (y_p.reshape(bsz, seq, D_MODEL), y_s.reshape(n_seq, n_new, D_MODEL),
            st["kvl_p"], st["kr_p"], st["ssm_p"], st["conv_p"], st["mk_p"], st["mv_p"],
            st["kvl_s"], st["kr_s"], st["ssm_s"], st["conv_s"], st["gv_s"])
```
